```python
import math
import jax, jax.numpy as jnp
from jax import lax
import numpy as np

D_MODEL = 1024
BATCH = 4
SEQ = 8192
DEPTH = 2

CHUNK = 128
EPS = 1e-6
SSD_HEADS = 16
SSD_HEAD_DIM = 64
SSD_WIDTH = SSD_HEADS * SSD_HEAD_DIM
SSD_GROUPS = 2
SSD_STATE = 128
CONV_WIDTH = 4
SSD_CONV_CH = SSD_WIDTH + 2 * SSD_GROUPS * SSD_STATE
SSD_IN = 2 * SSD_WIDTH + 2 * SSD_GROUPS * SSD_STATE + SSD_HEADS
S5_GROUP_CH = 16
S5_GROUPS = 32
S5_WIDTH = S5_GROUPS * S5_GROUP_CH
S5_STATE = 64
RET_HEADS = 8
RET_KEY_DIM = 32
RET_VAL_DIM = 64
RET_QK = RET_HEADS * RET_KEY_DIM
RET_WIDTH = RET_HEADS * RET_VAL_DIM
ROPE_BASE = 10000.0
MIX_WIDTH = SSD_WIDTH + S5_WIDTH + RET_WIDTH
IN_WIDTH = SSD_IN + 2 * S5_WIDTH + 2 * RET_QK + 2 * RET_WIDTH
SPLITS = [int(v) for v in np.cumsum([SSD_IN, S5_WIDTH, S5_WIDTH, RET_QK, RET_QK, RET_WIDTH])]

kernel_name = "hybrid_ssd_s5_retention_parallel_heads"


def _rmsnorm(x, w):
    xf = x.astype(jnp.float32)
    return xf * lax.rsqrt(jnp.mean(xf * xf, axis=-1, keepdims=True) + EPS) * w


def _causal_dwconv(x, w, b):
    y = lax.conv_general_dilated(
        x, w.astype(x.dtype)[:, None, :], window_strides=(1,),
        padding=[(CONV_WIDTH - 1, 0)], dimension_numbers=("NWC", "WIO", "NWC"),
        feature_group_count=x.shape[-1])
    return y + b


def _ssd_branch(p, conv_w, conv_b, dt_bias, a_log, d_skip, norm_w):
    b, l, _ = p.shape
    nc = l // CHUNK
    hg = SSD_HEADS // SSD_GROUPS
    z, xbc, dt = jnp.split(p, [SSD_WIDTH, SSD_WIDTH + SSD_CONV_CH], axis=-1)
    xbc = jax.nn.silu(_causal_dwconv(xbc, conv_w, conv_b))
    xs, bm, cm = jnp.split(xbc, [SSD_WIDTH, SSD_WIDTH + SSD_GROUPS * SSD_STATE], axis=-1)
    xs = xs.reshape(b, nc, CHUNK, SSD_GROUPS, hg, SSD_HEAD_DIM)
    bm = bm.reshape(b, nc, CHUNK, SSD_GROUPS, SSD_STATE)
    cm = cm.reshape(b, nc, CHUNK, SSD_GROUPS, SSD_STATE)
    dt = jax.nn.softplus(dt + dt_bias).reshape(b, nc, CHUNK, SSD_GROUPS, hg)
    a = -jnp.exp(a_log.astype(jnp.float32)).reshape(SSD_GROUPS, hg)
    xdt = xs * dt[..., None]
    acum = jnp.cumsum((dt * a).transpose(0, 3, 4, 1, 2), axis=-1)
    seg = acum[..., :, None] - acum[..., None, :]
    causal = jnp.tril(jnp.ones((CHUNK, CHUNK), dtype=bool))
    decay = jnp.exp(jnp.where(causal, seg, -jnp.inf))
    cb = jnp.einsum("bclgn,bcsgn->bgcls", cm, bm)
    y_diag = jnp.einsum("bgrcls,bcsgrp->bclgrp", cb[:, :, None] * decay, xdt)
    decay_states = jnp.exp(acum[..., -1:] - acum)
    states = jnp.einsum("bclgn,bgrcl,bclgrp->bcgrpn", bm, decay_states, xdt)
    chunk_decay = jnp.exp(acum[..., -1])

    def step(carry, inp):
        s_c, d_c = inp
        return carry * d_c[..., None, None] + s_c, carry

    init = jnp.zeros(states.shape[:1] + states.shape[2:], states.dtype)
    _, prev = lax.scan(step, init, (states.transpose(1, 0, 2, 3, 4, 5),
                                     chunk_decay.transpose(3, 0, 1, 2)))
    y_off = jnp.einsum("bclgn,cbgrpn->bclgrp", cm, prev) * \
        jnp.exp(acum).transpose(0, 3, 4, 1, 2)[..., None]
    y = y_diag + y_off + xs * d_skip.reshape(SSD_GROUPS, hg)[:, :, None]
    y = y.reshape(b, l, SSD_WIDTH) * jax.nn.silu(z)
    y = y.reshape(b, l, SSD_GROUPS, SSD_WIDTH // SSD_GROUPS)
    y = y * lax.rsqrt(jnp.mean(y * y, axis=-1, keepdims=True) + EPS)
    return y.reshape(b, l, SSD_WIDTH) * norm_w


def _s5_branch(u, lam_re, lam_im, b_re, b_im, c_re, c_im, d_skip, log_step, w_glu, b_glu):
    f32 = jnp.float32
    b, l, _ = u.shape
    lam_re = lam_re.astype(f32); lam_im = lam_im.astype(f32)
    step = jnp.exp(log_step.astype(f32))[:, None]
    mag = jnp.exp(lam_re * step)
    ang = lam_im * step
    lb_re = mag * jnp.cos(ang)
    lb_im = mag * jnp.sin(ang)
    den = lam_re * lam_re + lam_im * lam_im
    f_re = ((lb_re - 1.0) * lam_re + lb_im * lam_im) / den
    f_im = (lb_im * lam_re - (lb_re - 1.0) * lam_im) / den
    bb_re = f_re[..., None] * b_re - f_im[..., None] * b_im
    bb_im = f_re[..., None] * b_im + f_im[..., None] * b_re
    ug = u.reshape(b, l, S5_GROUPS, S5_GROUP_CH)
    bu_re = jnp.einsum("blgc,gpc->blgp", ug, bb_re)
    bu_im = jnp.einsum("blgc,gpc->blgp", ug, bb_im)
    a_re = jnp.broadcast_to(lb_re, (1, l, S5_GROUPS, S5_STATE))
    a_im = jnp.broadcast_to(lb_im, (1, l, S5_GROUPS, S5_STATE))

    def combine(ei, ej):
        ar_i, ai_i, br_i, bi_i = ei
        ar_j, ai_j, br_j, bi_j = ej
        return (ar_j * ar_i - ai_j * ai_i,
                ar_j * ai_i + ai_j * ar_i,
                ar_j * br_i - ai_j * bi_i + br_j,
                ar_j * bi_i + ai_j * br_i + bi_j)

    _, _, s_re, s_im = lax.associative_scan(combine, (a_re, a_im, bu_re, bu_im), axis=1)
    y = jnp.einsum("blgp,gcp->blgc", s_re, c_re) - jnp.einsum("blgp,gcp->blgc", s_im, c_im)
    y = y.reshape(b, l, S5_WIDTH) + d_skip * u
    y = jax.nn.gelu(y)
    return y * jax.nn.sigmoid(y @ w_glu + b_glu)


def _rotary(t, cos, sin):
    t1, t2 = jnp.split(t, 2, axis=-1)
    c = cos[:, None, :]
    s = sin[:, None, :]
    return jnp.concatenate([t1 * c - t2 * s, t1 * s + t2 * c], axis=-1)


def _retention_branch(q, k, v, norm_w):
    b, l, _ = q.shape
    nc = l // CHUNK
    pos = jnp.arange(l, dtype=jnp.float32)
    inv_freq = ROPE_BASE ** (-jnp.arange(0, RET_KEY_DIM, 2, dtype=jnp.float32) / RET_KEY_DIM)
    ang = pos[:, None] * inv_freq[None, :]
    cos, sin = jnp.cos(ang), jnp.sin(ang)
    q = _rotary(q.reshape(b, l, RET_HEADS, RET_KEY_DIM), cos, sin)
    k = _rotary(k.reshape(b, l, RET_HEADS, RET_KEY_DIM), cos, sin) * (RET_KEY_DIM ** -0.5)
    v = v.reshape(b, l, RET_HEADS, RET_VAL_DIM)
    log_g = jnp.log1p(-jnp.exp2(-5.0 - jnp.arange(RET_HEADS, dtype=jnp.float32)))
    q = q.reshape(b, nc, CHUNK, RET_HEADS, RET_KEY_DIM)
    k = k.reshape(b, nc, CHUNK, RET_HEADS, RET_KEY_DIM)
    v = v.reshape(b, nc, CHUNK, RET_HEADS, RET_VAL_DIM)
    idx = jnp.arange(CHUNK, dtype=jnp.float32)
    diff = idx[:, None] - idx[None, :]
    dmat = jnp.where(diff >= 0, jnp.exp(jnp.maximum(diff, 0.0) * log_g[:, None, None]), 0.0)
    scores = jnp.einsum("bcthd,bcshd->bchts", q, k) * dmat
    y_in = jnp.einsum("bchts,bcshe->bcthe", scores, v)
    k_dec = k * jnp.exp((CHUNK - 1.0 - idx)[:, None] * log_g[None, :])[..., None]
    kv = jnp.einsum("bcthd,bcthe->bchde", k_dec, v)
    chunk_decay = jnp.exp(CHUNK * log_g)

    def step(carry, kv_c):
        return carry * chunk_decay[:, None, None] + kv_c, carry

    init = jnp.zeros((b, RET_HEADS, RET_KEY_DIM, RET_VAL_DIM), kv.dtype)
    _, prev = lax.scan(step, init, kv.transpose(1, 0, 2, 3, 4))
    q_dec = q * jnp.exp((idx + 1.0)[:, None] * log_g[None, :])[..., None]
    y_cross = jnp.einsum("bcthd,cbhde->bcthe", q_dec, prev)
    y = (y_in + y_cross).reshape(b, l, RET_HEADS, RET_VAL_DIM)
    y = y * lax.rsqrt(jnp.mean(y * y, axis=-1, keepdims=True) + EPS)
    return y.reshape(b, l, RET_WIDTH) * norm_w


def setup_inputs(seed: int = 0) -> dict:
    key = jax.random.key(seed)
    ks = jax.random.split(key, 24)
    f32 = jnp.float32
    nrm = lambda k, s, sc: jax.random.normal(k, s, f32) * sc
    lo, hi = math.log(1e-3), math.log(1e-1)
    dt0 = jnp.exp(jax.random.uniform(ks[5], (DEPTH, SSD_HEADS), f32) * (hi - lo) + lo)
    inv_sqrt2 = 1.0 / math.sqrt(2.0)
    return {
        "x": nrm(ks[0], (BATCH, SEQ, D_MODEL), 1.0),
        "norm_w": 1.0 + nrm(ks[1], (DEPTH, D_MODEL), 0.02),
        "w_in": nrm(ks[2], (DEPTH, D_MODEL, IN_WIDTH), D_MODEL ** -0.5),
        "conv_w": nrm(ks[3], (DEPTH, CONV_WIDTH, SSD_CONV_CH), CONV_WIDTH ** -0.5),
        "conv_b": nrm(ks[4], (DEPTH, SSD_CONV_CH), 0.02),
        "dt_bias": dt0 + jnp.log(-jnp.expm1(-dt0)),
        "a_log": jnp.log(jax.random.uniform(ks[6], (DEPTH, SSD_HEADS), f32, 1.0, 16.0)),
        "d_ssd": 1.0 + nrm(ks[7], (DEPTH, SSD_HEADS), 0.02),
        "ssd_norm_w": 1.0 + nrm(ks[8], (DEPTH, SSD_WIDTH), 0.02),
        "s5_lambda_re": -0.5 + nrm(ks[9], (DEPTH, S5_GROUPS, S5_STATE), 0.01),
        "s5_lambda_im": math.pi * jnp.arange(S5_STATE, dtype=f32) + nrm(ks[10], (DEPTH, S5_GROUPS, S5_STATE), 0.01),
        "s5_b_re": nrm(ks[11], (DEPTH, S5_GROUPS, S5_STATE, S5_GROUP_CH), S5_GROUP_CH ** -0.5 * inv_sqrt2),
        "s5_b_im": nrm(ks[12], (DEPTH, S5_GROUPS, S5_STATE, S5_GROUP_CH), S5_GROUP_CH ** -0.5 * inv_sqrt2),
        "s5_c_re": nrm(ks[13], (DEPTH, S5_GROUPS, S5_GROUP_CH, S5_STATE), S5_STATE ** -0.5 * inv_sqrt2),
        "s5_c_im": nrm(ks[14], (DEPTH, S5_GROUPS, S5_GROUP_CH, S5_STATE), S5_STATE ** -0.5 * inv_sqrt2),
        "s5_d": nrm(ks[15], (DEPTH, S5_WIDTH), 1.0),
        "s5_log_step": jax.random.uniform(ks[16], (DEPTH, S5_GROUPS), f32) * (hi - lo) + lo,
        "s5_w_glu": nrm(ks[17], (DEPTH, S5_WIDTH, S5_WIDTH), S5_WIDTH ** -0.5),
        "s5_b_glu": nrm(ks[18], (DEPTH, S5_WIDTH), 0.02),
        "ret_norm_w": 1.0 + nrm(ks[19], (DEPTH, RET_WIDTH), 0.02),
        "w_out": nrm(ks[20], (DEPTH, MIX_WIDTH, D_MODEL), MIX_WIDTH ** -0.5),
        "final_norm_w": 1.0 + nrm(ks[21], (D_MODEL,), 0.02),
    }


def reference(x, norm_w, w_in, conv_w, conv_b, dt_bias, a_log, d_ssd, ssd_norm_w,
              s5_lambda_re, s5_lambda_im, s5_b_re, s5_b_im, s5_c_re, s5_c_im, s5_d,
              s5_log_step, s5_w_glu, s5_b_glu, ret_norm_w, w_out, final_norm_w):
    out_dtype = x.dtype
    h_res = x.astype(jnp.float32)
    for i in range(DEPTH):
        h = _rmsnorm(h_res, norm_w[i])
        proj = h @ w_in[i]
        p_ssd, s5_gate, s5_u, r_q, r_k, r_v, r_gate = jnp.split(proj, SPLITS, axis=-1)
        y_ssd = _ssd_branch(p_ssd, conv_w[i], conv_b[i], dt_bias[i], a_log[i], d_ssd[i], ssd_norm_w[i])
        y_s5 = _s5_branch(s5_u, s5_lambda_re[i], s5_lambda_im[i], s5_b_re[i], s5_b_im[i],
                          s5_c_re[i], s5_c_im[i], s5_d[i], s5_log_step[i], s5_w_glu[i],
                          s5_b_glu[i]) * jax.nn.silu(s5_gate)
        y_ret = _retention_branch(r_q, r_k, r_v, ret_norm_w[i]) * jax.nn.silu(r_gate)
        y = jnp.concatenate([y_ssd, y_s5, y_ret], axis=-1)
        h_res = h_res + y @ w_out[i]
    return _rmsnorm(h_res, final_norm_w).astype(out_dtype)
```

```python
import functools
import math

import numpy as np
import jax
import jax.numpy as jnp
from jax import lax
from jax.experimental import pallas as pl
from jax.experimental.pallas import tpu as pltpu

F32 = jnp.float32
BF16 = jnp.bfloat16

D_MODEL = 1024
CHUNK = 128
EPS = 1e-6
SSD_HEADS = 16
SSD_HEAD_DIM = 64
SSD_WIDTH = SSD_HEADS * SSD_HEAD_DIM
SSD_GROUPS = 2
SSD_STATE = 128
CONV_WIDTH = 4
SSD_BC = SSD_GROUPS * SSD_STATE
SSD_CONV_CH = SSD_WIDTH + 2 * SSD_BC
S5_GROUP_CH = 16
S5_GROUPS = 32
S5_WIDTH = S5_GROUPS * S5_GROUP_CH
S5_STATE = 64
S5_HALF_GROUPS = S5_GROUPS // 2
S5_HALF_CH = S5_WIDTH // 2
S5_HALF_STATE = S5_HALF_GROUPS * S5_STATE
S5_PLANE = S5_GROUPS * S5_STATE
S5_SUB = 16
RET_HEADS = 8
RET_KEY_DIM = 32
RET_VAL_DIM = 64
RET_QK = RET_HEADS * RET_KEY_DIM
RET_WIDTH = RET_HEADS * RET_VAL_DIM
ROPE_BASE = 10000.0
MIX_WIDTH = SSD_WIDTH + S5_WIDTH + RET_WIDTH
LANES = 128
DT_PAD = LANES
PIECES = (("z", SSD_WIDTH), ("xbc", SSD_CONV_CH), ("dt", DT_PAD), ("s5g", S5_WIDTH), ("s5u", S5_WIDTH),
          ("q", RET_QK), ("k", RET_QK), ("v", RET_WIDTH), ("rg", RET_WIDTH))
IN_ROWS = 512
VMEM_LIMIT = 56 * 1024 * 1024


def _silu(x):
    return x * (1.0 / (1.0 + jnp.exp(-x)))


def _sigmoid(x):
    return 1.0 / (1.0 + jnp.exp(-x))


def _split3(x):
    p1 = x.astype(BF16)
    r1 = x - p1.astype(F32)
    p2 = r1.astype(BF16)
    p3 = (r1 - p2.astype(F32)).astype(BF16)
    return p1, p2, p3


def _const_spec(shape):
    nd = len(shape)
    return pl.BlockSpec(shape, lambda *_: (0,) * nd, pipeline_mode=pl.Buffered(1))


def _chunk_spec(width):
    return pl.BlockSpec((None, CHUNK, width), lambda b, c: (b, c, 0))


def _inproj_kernel(x_ref, nw_ref, w_ref, *out_refs):
    x = x_ref[...]
    r = lax.rsqrt(jnp.mean(x * x, axis=-1, keepdims=True) + EPS)
    h = (x * r * nw_ref[...]).astype(BF16)
    off = 0
    for o in out_refs:
        width = o.shape[-1]
        o[...] = jnp.dot(h, w_ref[:, off:off + width], preferred_element_type=F32)
        off += width


def _inproj(x2, norm_w, w_packed):
    t = x2.shape[0]
    total = w_packed.shape[1]
    out_shape = [jax.ShapeDtypeStruct((t, w), F32) for _, w in PIECES]
    out_specs = [pl.BlockSpec((IN_ROWS, w), lambda i: (i, 0)) for _, w in PIECES]
    return pl.pallas_call(
        _inproj_kernel,
        grid=(t // IN_ROWS,),
        in_specs=[pl.BlockSpec((IN_ROWS, D_MODEL), lambda i: (i, 0)),
                  _const_spec((1, D_MODEL)),
                  _const_spec((D_MODEL, total))],
        out_specs=out_specs,
        out_shape=out_shape,
        compiler_params=pltpu.CompilerParams(dimension_semantics=("arbitrary",), vmem_limit_bytes=VMEM_LIMIT),
        name="inproj",
    )(x2, norm_w.reshape(1, D_MODEL), w_packed)


def _ssd_kernel(z_ref, xbc_ref, dt_ref, convw_ref, convb_ref, dtb_ref, alog_ref, dskip_ref, nw_ref,
                tri3_ref, expand_ref, pairmask_ref, out_ref, xpad_ref, state_ref):
    c = pl.program_id(1)

    @pl.when(c == 0)
    def _():
        state_ref[...] = jnp.zeros_like(state_ref)
        xpad_ref[0:8, :] = jnp.zeros((8, SSD_CONV_CH), F32)

    xpad_ref[8:8 + CHUNK, :] = xbc_ref[...]
    acc = jnp.broadcast_to(convb_ref[...], (CHUNK, SSD_CONV_CH))
    for k in range(CONV_WIDTH):
        off = 8 - (CONV_WIDTH - 1) + k
        acc = acc + convw_ref[k:k + 1, :] * xpad_ref[off:off + CHUNK, :]
    xpad_ref[0:8, :] = xpad_ref[CHUNK:CHUNK + 8, :]
    xbc = _silu(acc)
    xs = xbc[:, :SSD_WIDTH]
    bm = xbc[:, SSD_WIDTH:SSD_WIDTH + SSD_BC]
    cm = xbc[:, SSD_WIDTH + SSD_BC:]

    dtr = dt_ref[...] + dtb_ref[...]
    dt = jnp.maximum(dtr, 0.0) + jnp.log1p(jnp.exp(-jnp.abs(dtr)))
    a = -jnp.exp(alog_ref[...])
    dta = dt * a
    acum = jnp.dot(tri3_ref[...], jnp.concatenate(_split3(dta), axis=0), preferred_element_type=F32)
    acum_t = acum.T
    dt_t = dt.T
    ea = jnp.exp(acum)
    wgt = jnp.exp(acum[CHUNK - 1:CHUNK, :] - acum) * dt

    def expand(v):
        hi = v.astype(BF16)
        lo = (v - hi.astype(F32)).astype(BF16)
        return jnp.dot(jnp.concatenate([hi, lo], axis=1), expand_ref[...], preferred_element_type=F32)

    wgt_x = expand(wgt)
    ea_x = expand(ea)

    xs_b = xs.astype(BF16)
    xw_b = (xs * wgt_x).astype(BF16)
    row = lax.broadcasted_iota(jnp.int32, (CHUNK, CHUNK), 0)
    col = lax.broadcasted_iota(jnp.int32, (CHUNK, CHUNK), 1)
    causal = row >= col
    mask_l = pairmask_ref[0:1, :]
    mask_r = pairmask_ref[1:2, :]
    hg = SSD_HEADS // SSD_GROUPS
    gw = hg * SSD_HEAD_DIM
    y_parts = []
    s_parts = []
    for g in range(SSD_GROUPS):
        bm_g = bm[:, g * SSD_STATE:(g + 1) * SSD_STATE]
        cm_b = cm[:, g * SSD_STATE:(g + 1) * SSD_STATE].astype(BF16)
        bm_b = bm_g.astype(BF16)
        cb = lax.dot_general(cm_b, bm_b, (((1,), (1,)), ((), ())), preferred_element_type=F32)
        s_parts.append(jnp.dot(bm_g.T.astype(BF16), xw_b[:, g * gw:(g + 1) * gw], preferred_element_type=F32))
        y_off = jnp.dot(cm_b, state_ref[:, g * gw:(g + 1) * gw].astype(BF16), preferred_element_type=F32)
        scores = []
        for r in range(hg):
            h = g * hg + r
            seg = acum[:, h:h + 1] - acum_t[h:h + 1, :]
            decay = jnp.exp(jnp.where(causal, seg, -jnp.inf))
            scores.append((cb * decay * dt_t[h:h + 1, :]).astype(BF16))
        for j in range(hg // 2):
            lo_ = g * gw + j * LANES
            xp = xs_b[:, lo_:lo_ + LANES]
            rhs = jnp.concatenate([xp * mask_l, xp * mask_r], axis=0)
            lhs = jnp.concatenate([scores[2 * j], scores[2 * j + 1]], axis=1)
            y_parts.append(jnp.dot(lhs, rhs, preferred_element_type=F32)
                           + y_off[:, j * LANES:(j + 1) * LANES] * ea_x[:, lo_:lo_ + LANES])
    y = jnp.concatenate(y_parts, axis=1) + xs * dskip_ref[...]
    state_ref[...] = state_ref[...] * ea_x[CHUNK - 1:CHUNK, :] + jnp.concatenate(s_parts, axis=1)

    y = y * _silu(z_ref[...])
    outs = []
    for g in range(SSD_GROUPS):
        yg = y[:, g * gw:(g + 1) * gw]
        ms = jnp.mean(yg * yg, axis=-1, keepdims=True)
        outs.append(yg * lax.rsqrt(ms + EPS))
    out_ref[...] = (jnp.concatenate(outs, axis=1) * nw_ref[...]).astype(out_ref.dtype)


def _ssd_constants():
    tri = np.tril(np.ones((CHUNK, CHUNK), np.float32))
    tri3 = np.concatenate([tri, tri, tri], axis=1)
    expand = np.zeros((2 * LANES, SSD_WIDTH), np.float32)
    for h in range(SSD_HEADS):
        expand[h, h * SSD_HEAD_DIM:(h + 1) * SSD_HEAD_DIM] = 1.0
        expand[LANES + h, h * SSD_HEAD_DIM:(h + 1) * SSD_HEAD_DIM] = 1.0
    pair = np.zeros((2, LANES), np.float32)
    pair[0, :SSD_HEAD_DIM] = 1.0
    pair[1, SSD_HEAD_DIM:] = 1.0
    return jnp.asarray(tri3, BF16), jnp.asarray(expand, BF16), jnp.asarray(pair, BF16)


def _pad_lanes(v):
    return jnp.pad(v.reshape(1, -1), ((0, 0), (0, DT_PAD - v.shape[-1])))


def _ssd(z, xbc, dt, conv_w, conv_b, dt_bias, a_log, d_ssd, ssd_norm_w):
    b, l, _ = z.shape
    tri3, expand, pair = _ssd_constants()
    dskip = jnp.repeat(d_ssd, SSD_HEAD_DIM).reshape(1, SSD_WIDTH)
    params = [conv_w, conv_b.reshape(1, -1), _pad_lanes(dt_bias), _pad_lanes(a_log), dskip,
              ssd_norm_w.reshape(1, -1), tri3, expand, pair]
    return pl.pallas_call(
        _ssd_kernel,
        grid=(b, l // CHUNK),
        in_specs=[_chunk_spec(SSD_WIDTH), _chunk_spec(SSD_CONV_CH), _chunk_spec(DT_PAD)]
        + [_const_spec(p.shape) for p in params],
        out_specs=_chunk_spec(SSD_WIDTH),
        out_shape=jax.ShapeDtypeStruct((b, l, SSD_WIDTH), BF16),
        scratch_shapes=[pltpu.VMEM((CHUNK + 8, SSD_CONV_CH), F32),
                        pltpu.VMEM((SSD_STATE, SSD_WIDTH), F32)],
        compiler_params=pltpu.CompilerParams(dimension_semantics=("arbitrary", "arbitrary"),
                                             vmem_limit_bytes=VMEM_LIMIT),
        name="ssd",
    )(z, xbc, dt, *params)


def _s5_kernel(u_ref, gate_ref, bmat_ref, cmat_ref, dinv_re_ref, dinv_im_ref, dfwd_re_ref, dfwd_im_ref,
               tri_ref, dskip_ref, wglu_ref, bglu_ref, out_ref, carry_re_ref, carry_im_ref, x_ref):
    c = pl.program_id(1)

    @pl.when(c == 0)
    def _():
        carry_re_ref[...] = jnp.zeros_like(carry_re_ref)
        carry_im_ref[...] = jnp.zeros_like(carry_im_ref)

    u = u_ref[...]
    u_b = u.astype(BF16)
    bu_re, bu_im = [], []
    for a in range(2):
        bu = jnp.dot(u_b[:, a * S5_HALF_CH:(a + 1) * S5_HALF_CH], bmat_ref[a], preferred_element_type=F32)
        bu_re.append(bu[:, :S5_HALF_STATE])
        bu_im.append(bu[:, S5_HALF_STATE:])
    bu_re = jnp.concatenate(bu_re, axis=1)
    bu_im = jnp.concatenate(bu_im, axis=1)
    nsub = CHUNK // S5_SUB
    di_re = jnp.concatenate([dinv_re_ref[...]] * nsub, axis=0)
    di_im = jnp.concatenate([dinv_im_ref[...]] * nsub, axis=0)
    bs_re = (bu_re * di_re - bu_im * di_im).astype(BF16)
    bs_im = (bu_re * di_im + bu_im * di_re).astype(BF16)
    z_re = jnp.dot(tri_ref[...], bs_re, preferred_element_type=F32)
    z_im = jnp.dot(tri_ref[...], bs_im, preferred_element_type=F32)
    cr = carry_re_ref[...]
    ci = carry_im_ref[...]
    df_re = dfwd_re_ref[...]
    df_im = dfwd_im_ref[...]
    for m in range(nsub):
        zr = z_re[m * S5_SUB:(m + 1) * S5_SUB, :] + cr
        zi = z_im[m * S5_SUB:(m + 1) * S5_SUB, :] + ci
        xr = zr * df_re - zi * df_im
        xi = zr * df_im + zi * df_re
        cr = xr[S5_SUB - 1:S5_SUB, :]
        ci = xi[S5_SUB - 1:S5_SUB, :]
        x_ref[m * S5_SUB:(m + 1) * S5_SUB, :S5_PLANE] = xr.astype(BF16)
        x_ref[m * S5_SUB:(m + 1) * S5_SUB, S5_PLANE:] = xi.astype(BF16)
    carry_re_ref[...] = cr
    carry_im_ref[...] = ci
    ys = []
    for a in range(2):
        xa = jnp.concatenate([x_ref[:, a * S5_HALF_STATE:(a + 1) * S5_HALF_STATE],
                              x_ref[:, S5_PLANE + a * S5_HALF_STATE:S5_PLANE + (a + 1) * S5_HALF_STATE]], axis=1)
        ys.append(jnp.dot(xa, cmat_ref[a], preferred_element_type=F32))
    y = jnp.concatenate(ys, axis=1) + dskip_ref[...] * u
    y = 0.5 * y * (1.0 + jnp.tanh(math.sqrt(2.0 / math.pi) * (y + 0.044715 * (y * y * y))))
    glu = jnp.dot(y.astype(BF16), wglu_ref[...], preferred_element_type=F32) + bglu_ref[...]
    out_ref[...] = (y * _sigmoid(glu) * _silu(gate_ref[...])).astype(out_ref.dtype)


def _s5_tables(lam_re, lam_im, b_re, b_im, c_re, c_im, log_step):
    step = jnp.exp(log_step.astype(F32))[:, None]
    lam_re = lam_re.astype(F32)
    lam_im = lam_im.astype(F32)
    mag = jnp.exp(lam_re * step)
    ang = lam_im * step
    lb_re = mag * jnp.cos(ang)
    lb_im = mag * jnp.sin(ang)
    den = lam_re * lam_re + lam_im * lam_im
    f_re = ((lb_re - 1.0) * lam_re + lb_im * lam_im) / den
    f_im = (lb_im * lam_re - (lb_re - 1.0) * lam_im) / den
    bb_re = f_re[..., None] * b_re - f_im[..., None] * b_im
    bb_im = f_re[..., None] * b_im + f_im[..., None] * b_re
    eye = jnp.eye(S5_HALF_GROUPS, dtype=F32)

    def in_block(bb):
        bb = bb.reshape(2, S5_HALF_GROUPS, S5_STATE, S5_GROUP_CH)
        return jnp.einsum("agpc,gh->agchp", bb, eye).reshape(2, S5_HALF_CH, S5_HALF_STATE)

    def out_block(cc):
        cc = cc.reshape(2, S5_HALF_GROUPS, S5_GROUP_CH, S5_STATE)
        return jnp.einsum("agcp,gh->agphc", cc, eye).reshape(2, S5_HALF_STATE, S5_HALF_CH)

    bmat = jnp.concatenate([in_block(bb_re), in_block(bb_im)], axis=2).astype(BF16)
    cmat = jnp.concatenate([out_block(c_re), out_block(-c_im)], axis=1).astype(BF16)
    k = jnp.arange(1, S5_SUB + 1, dtype=F32)[:, None, None]
    la = (lam_re * step)[None]
    an = ang[None]
    dfwd_re = (jnp.exp(k * la) * jnp.cos(k * an)).reshape(S5_SUB, S5_PLANE)
    dfwd_im = (jnp.exp(k * la) * jnp.sin(k * an)).reshape(S5_SUB, S5_PLANE)
    dinv_re = (jnp.exp(-k * la) * jnp.cos(k * an)).reshape(S5_SUB, S5_PLANE)
    dinv_im = (-jnp.exp(-k * la) * jnp.sin(k * an)).reshape(S5_SUB, S5_PLANE)
    return bmat, cmat, dinv_re, dinv_im, dfwd_re, dfwd_im


def _s5(u, gate, lam_re, lam_im, b_re, b_im, c_re, c_im, d_skip, log_step, w_glu, b_glu):
    b, l, _ = u.shape
    bmat, cmat, dinv_re, dinv_im, dfwd_re, dfwd_im = _s5_tables(lam_re, lam_im, b_re, b_im, c_re, c_im, log_step)
    blk = np.kron(np.eye(CHUNK // S5_SUB, dtype=np.float32), np.tril(np.ones((S5_SUB, S5_SUB), np.float32)))
    params = [bmat, cmat, dinv_re, dinv_im, dfwd_re, dfwd_im, jnp.asarray(blk, BF16),
              d_skip.reshape(1, -1), w_glu.astype(BF16), b_glu.reshape(1, -1)]
    return pl.pallas_call(
        _s5_kernel,
        grid=(b, l // CHUNK),
        in_specs=[_chunk_spec(S5_WIDTH), _chunk_spec(S5_WIDTH)] + [_const_spec(p.shape) for p in params],
        out_specs=_chunk_spec(S5_WIDTH),
        out_shape=jax.ShapeDtypeStruct((b, l, S5_WIDTH), BF16),
        scratch_shapes=[pltpu.VMEM((1, S5_PLANE), F32), pltpu.VMEM((1, S5_PLANE), F32),
                        pltpu.VMEM((CHUNK, 2 * S5_PLANE), BF16)],
        compiler_params=pltpu.CompilerParams(dimension_semantics=("arbitrary", "arbitrary"),
                                             vmem_limit_bytes=VMEM_LIMIT),
        name="s5",
    )(u, gate, *params)


def _ret_kernel(q_ref, k_ref, v_ref, gate_ref, cos_ref, sin_ref, qdec_ref, kdect_ref, dmat_ref, headmask_ref,
                kvmask_ref, cdec_ref, pairmask_ref, seg_ref, nw_ref, out_ref, state_ref):
    c = pl.program_id(1)

    @pl.when(c == 0)
    def _():
        state_ref[...] = jnp.zeros_like(state_ref)

    half = RET_KEY_DIM // 2
    lane = lax.broadcasted_iota(jnp.int32, (CHUNK, RET_QK), 1)
    first_half = (lane % RET_KEY_DIM) < half
    cos = cos_ref[...]
    sin = sin_ref[...]

    def rotary(t):
        swapped = jnp.where(first_half, pltpu.roll(t, RET_QK - half, 1), pltpu.roll(t, half, 1))
        return t * cos + swapped * sin

    q = rotary(q_ref[...])
    k = rotary(k_ref[...]) * (RET_KEY_DIM ** -0.5)
    v = v_ref[...]
    v_b = v.astype(BF16)
    k_t = k.T
    kt_b = k_t.astype(BF16)
    q_b = q.astype(BF16)
    q_heads = jnp.concatenate([q_b * headmask_ref[h:h + 1, :] for h in range(RET_HEADS)], axis=0)
    scores = jnp.dot(q_heads, kt_b, preferred_element_type=F32)
    y_cross = jnp.dot((q * qdec_ref[...]).astype(BF16), state_ref[...].astype(BF16), preferred_element_type=F32)
    mask_l = pairmask_ref[0:1, :]
    mask_r = pairmask_ref[1:2, :]
    y_parts = []
    for j in range(RET_HEADS // 2):
        sa = (scores[(2 * j) * CHUNK:(2 * j + 1) * CHUNK, :] * dmat_ref[2 * j]).astype(BF16)
        sb = (scores[(2 * j + 1) * CHUNK:(2 * j + 2) * CHUNK, :] * dmat_ref[2 * j + 1]).astype(BF16)
        vp = v_b[:, j * LANES:(j + 1) * LANES]
        rhs = jnp.concatenate([vp * mask_l, vp * mask_r], axis=0)
        y_parts.append(jnp.dot(jnp.concatenate([sa, sb], axis=1), rhs, preferred_element_type=F32))
    y = jnp.concatenate(y_parts, axis=1) + y_cross
    kv = jnp.dot((k_t * kdect_ref[...]).astype(BF16), v_b, preferred_element_type=F32)
    state_ref[...] = state_ref[...] * cdec_ref[...] + kv * kvmask_ref[...]
    ms = jnp.dot((y * y).astype(BF16), seg_ref[...], preferred_element_type=F32)
    out = y * lax.rsqrt(ms + EPS) * nw_ref[...] * _silu(gate_ref[...])
    out_ref[...] = out.astype(out_ref.dtype)


def _ret_constants(l):
    half = RET_KEY_DIM // 2
    pos = jnp.arange(l, dtype=F32)
    inv_freq = ROPE_BASE ** (-jnp.arange(0, RET_KEY_DIM, 2, dtype=F32) / RET_KEY_DIM)
    ang = pos[:, None] * inv_freq[None, :]
    cos, sin = jnp.cos(ang), jnp.sin(ang)
    cos_t = jnp.tile(jnp.concatenate([cos, cos], axis=1), (1, RET_HEADS))
    sin_t = jnp.tile(jnp.concatenate([-sin, sin], axis=1), (1, RET_HEADS))
    log_g = jnp.log1p(-jnp.exp2(-5.0 - jnp.arange(RET_HEADS, dtype=F32)))
    idx = jnp.arange(CHUNK, dtype=F32)
    diff = idx[:, None] - idx[None, :]
    dmat = jnp.where(diff >= 0, jnp.exp(jnp.maximum(diff, 0.0) * log_g[:, None, None]), 0.0)
    qdec = jnp.repeat(jnp.exp((idx + 1.0)[:, None] * log_g[None, :]), RET_KEY_DIM, axis=1)
    kdect = jnp.repeat(jnp.exp((CHUNK - 1.0 - idx)[:, None] * log_g[None, :]), RET_KEY_DIM, axis=1).T
    cdec = jnp.repeat(jnp.exp(CHUNK * log_g), RET_VAL_DIM).reshape(1, RET_WIDTH)
    headmask = np.kron(np.eye(RET_HEADS, dtype=np.float32), np.ones((1, RET_KEY_DIM), np.float32))
    kvmask = np.kron(np.eye(RET_HEADS, dtype=np.float32), np.ones((RET_KEY_DIM, RET_VAL_DIM), np.float32))
    pair = np.zeros((2, LANES), np.float32)
    pair[0, :RET_VAL_DIM] = 1.0
    pair[1, RET_VAL_DIM:] = 1.0
    seg = np.kron(np.eye(RET_HEADS, dtype=np.float32),
                  np.full((RET_VAL_DIM, RET_VAL_DIM), 1.0 / RET_VAL_DIM, np.float32))
    return (cos_t, sin_t, qdec, kdect, dmat, jnp.asarray(headmask, BF16), jnp.asarray(kvmask, F32), cdec,
            jnp.asarray(pair, BF16), jnp.asarray(seg, BF16))


def _ret(q, k, v, gate, ret_norm_w):
    b, l, _ = q.shape
    cos_t, sin_t, qdec, kdect, dmat, headmask, kvmask, cdec, pair, seg = _ret_constants(l)
    params = [qdec, kdect, dmat, headmask, kvmask, cdec, pair, seg, ret_norm_w.reshape(1, -1)]
    pos_spec = pl.BlockSpec((CHUNK, RET_QK), lambda b_, c: (c, 0))
    return pl.pallas_call(
        _ret_kernel,
        grid=(b, l // CHUNK),
        in_specs=[_chunk_spec(RET_QK), _chunk_spec(RET_QK), _chunk_spec(RET_WIDTH), _chunk_spec(RET_WIDTH),
                  pos_spec, pos_spec] + [_const_spec(p.shape) for p in params],
        out_specs=_chunk_spec(RET_WIDTH),
        out_shape=jax.ShapeDtypeStruct((b, l, RET_WIDTH), BF16),
        scratch_shapes=[pltpu.VMEM((RET_QK, RET_WIDTH), F32)],
        compiler_params=pltpu.CompilerParams(dimension_semantics=("arbitrary", "arbitrary"),
                                             vmem_limit_bytes=VMEM_LIMIT),
        name="retention",
    )(q, k, v, gate, cos_t, sin_t, *params)


def _outproj_kernel(final, yssd_ref, ys5_ref, yret_ref, res_ref, w_ref, fnw_ref, out_ref):
    acc = res_ref[...]
    acc = acc + jnp.dot(yssd_ref[...], w_ref[0:SSD_WIDTH, :], preferred_element_type=F32)
    acc = acc + jnp.dot(ys5_ref[...], w_ref[SSD_WIDTH:SSD_WIDTH + S5_WIDTH, :], preferred_element_type=F32)
    acc = acc + jnp.dot(yret_ref[...], w_ref[SSD_WIDTH + S5_WIDTH:, :], preferred_element_type=F32)
    if final:
        r = lax.rsqrt(jnp.mean(acc * acc, axis=-1, keepdims=True) + EPS)
        acc = acc * r * fnw_ref[...]
    out_ref[...] = acc


def _outproj(y_ssd, y_s5, y_ret, res, w_out, final_norm_w, final):
    t = res.shape[0]
    row = lambda w: pl.BlockSpec((IN_ROWS, w), lambda i: (i, 0))
    return pl.pallas_call(
        functools.partial(_outproj_kernel, final),
        grid=(t // IN_ROWS,),
        in_specs=[row(SSD_WIDTH), row(S5_WIDTH), row(RET_WIDTH), row(D_MODEL),
                  _const_spec((MIX_WIDTH, D_MODEL)), _const_spec((1, D_MODEL))],
        out_specs=row(D_MODEL),
        out_shape=jax.ShapeDtypeStruct((t, D_MODEL), F32),
        compiler_params=pltpu.CompilerParams(dimension_semantics=("arbitrary",), vmem_limit_bytes=VMEM_LIMIT),
        name="outproj",
    )(y_ssd, y_s5, y_ret, res, w_out.astype(BF16), final_norm_w.reshape(1, D_MODEL))


def _pack_w_in(w):
    o = SSD_WIDTH + SSD_CONV_CH
    n_dt = SSD_HEADS
    return jnp.concatenate([w[:, :o + n_dt], jnp.zeros((w.shape[0], DT_PAD - n_dt), w.dtype), w[:, o + n_dt:]],
                           axis=1).astype(BF16)


def kernel(x, norm_w, w_in, conv_w, conv_b, dt_bias, a_log, d_ssd, ssd_norm_w, s5_lambda_re, s5_lambda_im,
           s5_b_re, s5_b_im, s5_c_re, s5_c_im, s5_d, s5_log_step, s5_w_glu, s5_b_glu, ret_norm_w, w_out,
           final_norm_w):
    b, l, d = x.shape
    depth = w_in.shape[0]
    t = b * l
    h_res = x.astype(F32).reshape(t, d)
    for i in range(depth):
        z, xbc, dt, s5g, s5u, q, k, v, rg = _inproj(h_res, norm_w[i], _pack_w_in(w_in[i]))
        r3 = lambda arr: arr.reshape(b, l, arr.shape[-1])
        y_ssd = _ssd(r3(z), r3(xbc), r3(dt), conv_w[i], conv_b[i], dt_bias[i], a_log[i], d_ssd[i], ssd_norm_w[i])
        y_s5 = _s5(r3(s5u), r3(s5g), s5_lambda_re[i], s5_lambda_im[i], s5_b_re[i], s5_b_im[i], s5_c_re[i],
                   s5_c_im[i], s5_d[i], s5_log_step[i], s5_w_glu[i], s5_b_glu[i])
        y_ret = _ret(r3(q), r3(k), r3(v), r3(rg), ret_norm_w[i])
        h_res = _outproj(y_ssd.reshape(t, -1), y_s5.reshape(t, -1), y_ret.reshape(t, -1), h_res, w_out[i],
                         final_norm_w, final=(i == depth - 1))
    return h_res.reshape(b, l, d).astype(x.dtype)
```

```python
import functools
import math

import numpy as np
import jax
import jax.numpy as jnp
from jax import lax
from jax.experimental import pallas as pl
from jax.experimental.pallas import tpu as pltpu

F32 = jnp.float32
BF16 = jnp.bfloat16

D_MODEL = 1024
CHUNK = 128
EPS = 1e-6
SSD_HEADS = 16
SSD_HEAD_DIM = 64
SSD_WIDTH = SSD_HEADS * SSD_HEAD_DIM
SSD_GROUPS = 2
SSD_STATE = 128
CONV_WIDTH = 4
SSD_BC = SSD_GROUPS * SSD_STATE
SSD_CONV_CH = SSD_WIDTH + 2 * SSD_BC
S5_GROUP_CH = 16
S5_GROUPS = 32
S5_WIDTH = S5_GROUPS * S5_GROUP_CH
S5_STATE = 64
S5_HALF_GROUPS = S5_GROUPS // 2
S5_HALF_CH = S5_WIDTH // 2
S5_HALF_STATE = S5_HALF_GROUPS * S5_STATE
S5_PLANE = S5_GROUPS * S5_STATE
S5_SUB = 16
RET_HEADS = 8
RET_KEY_DIM = 32
RET_VAL_DIM = 64
RET_QK = RET_HEADS * RET_KEY_DIM
RET_WIDTH = RET_HEADS * RET_VAL_DIM
ROPE_BASE = 10000.0
MIX_WIDTH = SSD_WIDTH + S5_WIDTH + RET_WIDTH
LANES = 128
SUBLANES = 8
DT_PAD = LANES
PIECES = (("z", SSD_WIDTH), ("xbc", SSD_CONV_CH), ("dt", DT_PAD), ("s5g", S5_WIDTH), ("s5u", S5_WIDTH),
          ("q", RET_QK), ("k", RET_QK), ("v", RET_WIDTH), ("rg", RET_WIDTH))
PIECE_OFF = {}
_o = 0
for _n, _w in PIECES:
    PIECE_OFF[_n] = (_o, _w)
    _o += _w
IN_PACKED = _o
VMEM_LIMIT = 56 * 1024 * 1024


def _silu(x):
    return x * (1.0 / (1.0 + jnp.exp(-x)))


def _sigmoid(x):
    return 1.0 / (1.0 + jnp.exp(-x))


def _split3(x):
    p1 = x.astype(BF16)
    r1 = x - p1.astype(F32)
    p2 = r1.astype(BF16)
    p3 = (r1 - p2.astype(F32)).astype(BF16)
    return p1, p2, p3


def _pair_rhs(xp, mask_l, mask_r):
    return jnp.concatenate([(xp * mask_l).astype(BF16), (xp * mask_r).astype(BF16)], axis=0)


def _ssd_chunk(fill, z, xbc_raw, dt_raw, convw_ref, convb_ref, dtb_ref, alog_ref, dskip_ref, nw_ref,
               tri3_ref, expand_ref, mask_l, mask_r, xprev_ref, state_ref):
    xprev = xprev_ref[...]
    row8 = lax.broadcasted_iota(jnp.int32, (SUBLANES, UNIT_COLS), 0)
    blocks = []
    for c0 in range(0, SSD_CONV_CH, UNIT_COLS):
        cols = slice(c0, c0 + UNIT_COLS)
        xb = xbc_raw[:, cols]
        acc = convb_ref[:, cols] + convw_ref[CONV_WIDTH - 1:CONV_WIDTH, cols] * xb
        for j in range(1, CONV_WIDTH):
            rolled = pltpu.roll(xb, j, 0)
            head = jnp.where(row8 < j, pltpu.roll(xprev[:, cols], j, 0), rolled[0:SUBLANES, :])
            shifted = jnp.concatenate([head, rolled[SUBLANES:, :]], axis=0)
            acc = acc + convw_ref[CONV_WIDTH - 1 - j:CONV_WIDTH - j, cols] * shifted
        blocks.append(_silu(acc))
        fill(1)
    xprev_ref[...] = xbc_raw[CHUNK - SUBLANES:, :]
    xbc = jnp.concatenate(blocks, axis=1)
    xs = xbc[:, :SSD_WIDTH]
    bm = xbc[:, SSD_WIDTH:SSD_WIDTH + SSD_BC]
    cm = xbc[:, SSD_WIDTH + SSD_BC:]

    dtr = dt_raw + dtb_ref[...]
    dt = jnp.maximum(dtr, 0.0) + jnp.log1p(jnp.exp(-jnp.abs(dtr)))
    a = -jnp.exp(alog_ref[...])
    dta = dt * a
    acum = jnp.dot(tri3_ref[...], jnp.concatenate(_split3(dta), axis=0), preferred_element_type=F32)
    acum_t = acum.T
    dt_t = dt.T
    ea = jnp.exp(acum)
    wgt = jnp.exp(acum[CHUNK - 1:CHUNK, :] - acum) * dt

    def expand(v):
        hi = v.astype(BF16)
        lo = (v - hi.astype(F32)).astype(BF16)
        return jnp.dot(jnp.concatenate([hi, lo], axis=1), expand_ref[...], preferred_element_type=F32)

    wgt_x = expand(wgt)
    ea_x = expand(ea)

    xw_b = (xs * wgt_x).astype(BF16)
    row = lax.broadcasted_iota(jnp.int32, (CHUNK, CHUNK), 0)
    col = lax.broadcasted_iota(jnp.int32, (CHUNK, CHUNK), 1)
    causal = row >= col
    hg = SSD_HEADS // SSD_GROUPS
    gw = hg * SSD_HEAD_DIM
    y_parts = []
    s_parts = []
    for g in range(SSD_GROUPS):
        bm_g = bm[:, g * SSD_STATE:(g + 1) * SSD_STATE]
        cm_b = cm[:, g * SSD_STATE:(g + 1) * SSD_STATE].astype(BF16)
        bm_b = bm_g.astype(BF16)
        cb = lax.dot_general(cm_b, bm_b, (((1,), (1,)), ((), ())), preferred_element_type=F32)
        s_parts.append(jnp.dot(bm_g.T.astype(BF16), xw_b[:, g * gw:(g + 1) * gw], preferred_element_type=F32))
        y_off = jnp.dot(cm_b, state_ref[:, g * gw:(g + 1) * gw].astype(BF16), preferred_element_type=F32)
        scores = []
        for r in range(hg):
            h = g * hg + r
            seg = acum[:, h:h + 1] - acum_t[h:h + 1, :]
            decay = jnp.exp(jnp.where(causal, seg, -jnp.inf))
            scores.append((cb * decay * dt_t[h:h + 1, :]).astype(BF16))
            if r % 2 == 1:
                j = r // 2
                lo_ = g * gw + j * LANES
                rhs = _pair_rhs(xs[:, lo_:lo_ + LANES], mask_l, mask_r)
                lhs = jnp.concatenate([scores[2 * j], scores[2 * j + 1]], axis=1)
                y_parts.append(jnp.dot(lhs, rhs, preferred_element_type=F32)
                               + y_off[:, j * LANES:(j + 1) * LANES] * ea_x[:, lo_:lo_ + LANES])
                fill(1)
    state_ref[...] = state_ref[...] * ea_x[CHUNK - 1:CHUNK, :] + jnp.concatenate(s_parts, axis=1)

    outs = []
    for g in range(SSD_GROUPS):
        cols = slice(g * gw, (g + 1) * gw)
        yg = jnp.concatenate(y_parts[g * (hg // 2):(g + 1) * (hg // 2)], axis=1) + xs[:, cols] * dskip_ref[:, cols]
        yg = yg * _silu(z[:, cols])
        ms = jnp.mean(yg * yg, axis=-1, keepdims=True)
        outs.append(yg * lax.rsqrt(ms + EPS) * nw_ref[:, cols])
        fill(1)
    return jnp.concatenate(outs, axis=1).astype(BF16)


def _s5_chunk(fill, u, gate, bmat_ref, cmat_ref, dinv_re_ref, dinv_im_ref, dfwd_re_ref, dfwd_im_ref,
              tri_ref, dskip_ref, wglu_ref, bglu_ref, carry_re_ref, carry_im_ref, x_ref):
    u_b = u.astype(BF16)
    bu_re, bu_im = [], []
    for a in range(2):
        bu = jnp.dot(u_b[:, a * S5_HALF_CH:(a + 1) * S5_HALF_CH], bmat_ref[a], preferred_element_type=F32)
        bu_re.append(bu[:, :S5_HALF_STATE])
        bu_im.append(bu[:, S5_HALF_STATE:])
    bu_re = jnp.concatenate(bu_re, axis=1)
    bu_im = jnp.concatenate(bu_im, axis=1)
    nsub = CHUNK // S5_SUB
    di_re = jnp.concatenate([dinv_re_ref[...]] * nsub, axis=0)
    di_im = jnp.concatenate([dinv_im_ref[...]] * nsub, axis=0)
    bs_re = (bu_re * di_re - bu_im * di_im).astype(BF16)
    bs_im = (bu_re * di_im + bu_im * di_re).astype(BF16)
    fill(1)
    z_re = jnp.dot(tri_ref[...], bs_re, preferred_element_type=F32)
    z_im = jnp.dot(tri_ref[...], bs_im, preferred_element_type=F32)
    cr = carry_re_ref[...]
    ci = carry_im_ref[...]
    df_re = dfwd_re_ref[...]
    df_im = dfwd_im_ref[...]
    for m in range(nsub):
        zr = z_re[m * S5_SUB:(m + 1) * S5_SUB, :] + cr
        zi = z_im[m * S5_SUB:(m + 1) * S5_SUB, :] + ci
        xr = zr * df_re - zi * df_im
        xi = zr * df_im + zi * df_re
        cr = xr[S5_SUB - 1:S5_SUB, :]
        ci = xi[S5_SUB - 1:S5_SUB, :]
        x_ref[m * S5_SUB:(m + 1) * S5_SUB, :S5_PLANE] = xr.astype(BF16)
        x_ref[m * S5_SUB:(m + 1) * S5_SUB, S5_PLANE:] = xi.astype(BF16)
        if m % 4 == 3:
            fill(1)
    carry_re_ref[...] = cr
    carry_im_ref[...] = ci
    ys = []
    for a in range(2):
        xa = jnp.concatenate([x_ref[:, a * S5_HALF_STATE:(a + 1) * S5_HALF_STATE],
                              x_ref[:, S5_PLANE + a * S5_HALF_STATE:S5_PLANE + (a + 1) * S5_HALF_STATE]], axis=1)
        ys.append(jnp.dot(xa, cmat_ref[a], preferred_element_type=F32))
    y = jnp.concatenate(ys, axis=1) + dskip_ref[...] * u
    y = 0.5 * y * (1.0 + jnp.tanh(math.sqrt(2.0 / math.pi) * (y + 0.044715 * (y * y * y))))
    glu = jnp.dot(y.astype(BF16), wglu_ref[...], preferred_element_type=F32) + bglu_ref[...]
    return (y * _sigmoid(glu) * _silu(gate)).astype(BF16)


def _ret_chunk(fill, q_raw, k_raw, v, gate, cos_ref, sin_ref, qdec_ref, kdect_ref, dmat_ref, headmask_ref,
               kvmask_ref, cdec_ref, mask_l, mask_r, seg_ref, nw_ref, state_ref):
    half = RET_KEY_DIM // 2
    lane = lax.broadcasted_iota(jnp.int32, (CHUNK, RET_QK), 1)
    first_half = (lane % RET_KEY_DIM) < half
    cos = cos_ref[...]
    sin = sin_ref[...]

    def rotary(t):
        swapped = jnp.where(first_half, pltpu.roll(t, RET_QK - half, 1), pltpu.roll(t, half, 1))
        return t * cos + swapped * sin

    q = rotary(q_raw)
    fill(1)
    k = rotary(k_raw) * (RET_KEY_DIM ** -0.5)
    fill(1)
    v_b = v.astype(BF16)
    k_t = k.T
    q_b = q.astype(BF16)
    q_heads = jnp.concatenate([q_b * headmask_ref[h:h + 1, :] for h in range(RET_HEADS)], axis=0)
    scores = jnp.dot(q_heads, k_t.astype(BF16), preferred_element_type=F32)
    y_cross = jnp.dot((q * qdec_ref[...]).astype(BF16), state_ref[...].astype(BF16), preferred_element_type=F32)
    y_parts = []
    for j in range(RET_HEADS // 2):
        sa = (scores[(2 * j) * CHUNK:(2 * j + 1) * CHUNK, :] * dmat_ref[2 * j]).astype(BF16)
        sb = (scores[(2 * j + 1) * CHUNK:(2 * j + 2) * CHUNK, :] * dmat_ref[2 * j + 1]).astype(BF16)
        rhs = _pair_rhs(v[:, j * LANES:(j + 1) * LANES], mask_l, mask_r)
        y_parts.append(jnp.dot(jnp.concatenate([sa, sb], axis=1), rhs, preferred_element_type=F32))
        if j % 2 == 1:
            fill(1)
    y = jnp.concatenate(y_parts, axis=1) + y_cross
    kv = jnp.dot((k_t * kdect_ref[...]).astype(BF16), v_b, preferred_element_type=F32)
    state_ref[...] = state_ref[...] * cdec_ref[...] + kv * kvmask_ref[...]
    ms = jnp.dot((y * y).astype(BF16), seg_ref[...], preferred_element_type=F32)
    return (y * lax.rsqrt(ms + EPS) * nw_ref[...] * _silu(gate)).astype(BF16)


N_SSD, N_S5, N_RET = 9, 10, 8


class _Filler:
    def __init__(self):
        self.units = []

    def add(self, fn):
        self.units.append(fn)

    def __call__(self, n=1):
        for _ in range(min(n, len(self.units))):
            self.units.pop(0)()


UNIT_COLS = 256


def _layer_kernel(final, nc, xin_ref, xres_ref, nw_ref, win_ref, cos_ref, sin_ref, *rest):
    ssd_p = rest[:N_SSD]
    s5_p = rest[N_SSD:N_SSD + N_S5]
    ret_p = rest[N_SSD + N_S5:N_SSD + N_S5 + N_RET]
    wout_ref, fnw_ref, out_ref = rest[N_SSD + N_S5 + N_RET:N_SSD + N_S5 + N_RET + 3]
    scratch = rest[N_SSD + N_S5 + N_RET + 3:]
    proj = dict(zip([n for n, _ in PIECES], scratch[:len(PIECES)]))
    y_ref, h_ref, xprev_ref, ssd_state_ref, carry_re_ref, carry_im_ref, s5x_ref, ret_state_ref = scratch[len(PIECES):]
    s = pl.program_id(0)
    cur = (s + 1) % 2
    nxt = s % 2

    @pl.when(s == 0)
    def _():
        for ref in list(proj.values()) + [y_ref]:
            ref[...] = jnp.zeros_like(ref)

    @pl.when(jnp.logical_or(s == 0, (s + nc - 1) % nc == 0))
    def _():
        for ref in (xprev_ref, ssd_state_ref, carry_re_ref, carry_im_ref, ret_state_ref):
            ref[...] = jnp.zeros_like(ref)

    x = xin_ref[...]
    r = lax.rsqrt(jnp.mean(x * x, axis=-1, keepdims=True) + EPS)
    h_ref[...] = (x * r * nw_ref[...]).astype(BF16)

    fill = _Filler()

    def out_unit(c0):
        def run():
            out_ref[:, c0:c0 + UNIT_COLS] = xres_ref[:, c0:c0 + UNIT_COLS] + jnp.dot(
                y_ref[nxt], wout_ref[:, c0:c0 + UNIT_COLS], preferred_element_type=F32)
        return run

    def in_unit(name, c0, c1):
        off, _ = PIECE_OFF[name]

        def run():
            proj[name][nxt, :, c0:c1] = jnp.dot(h_ref[...], win_ref[:, off + c0:off + c1],
                                                preferred_element_type=F32)
        return run

    for c0 in range(0, D_MODEL, UNIT_COLS):
        fill.add(out_unit(c0))
    for name, width in PIECES:
        for c0 in range(0, width, UNIT_COLS):
            fill.add(in_unit(name, c0, min(c0 + UNIT_COLS, width)))

    convw, convb, dtb, alog, dskip, ssd_nw, tri3, expand, pairmask = ssd_p
    mask_l = pairmask[0:1, :]
    mask_r = pairmask[1:2, :]
    y_ref[cur, :, 0:SSD_WIDTH] = _ssd_chunk(fill, proj["z"][cur], proj["xbc"][cur], proj["dt"][cur], convw, convb,
                                            dtb, alog, dskip, ssd_nw, tri3, expand, mask_l, mask_r, xprev_ref,
                                            ssd_state_ref)
    qdec, kdect, dmat, headmask, kvmask, cdec, seg, ret_nw = ret_p
    y_ref[cur, :, SSD_WIDTH + S5_WIDTH:] = _ret_chunk(fill, proj["q"][cur], proj["k"][cur], proj["v"][cur],
                                                      proj["rg"][cur], cos_ref, sin_ref, qdec, kdect, dmat, headmask,
                                                      kvmask, cdec, mask_l, mask_r, seg, ret_nw, ret_state_ref)
    y_ref[cur, :, SSD_WIDTH:SSD_WIDTH + S5_WIDTH] = _s5_chunk(fill, proj["s5u"][cur], proj["s5g"][cur], *s5_p,
                                                              carry_re_ref, carry_im_ref, s5x_ref)
    fill(len(fill.units))
    if final:
        acc = out_ref[...]
        rr = lax.rsqrt(jnp.mean(acc * acc, axis=-1, keepdims=True) + EPS)
        out_ref[...] = acc * rr * fnw_ref[...]


def _ssd_constants():
    tri = np.tril(np.ones((CHUNK, CHUNK), np.float32))
    tri3 = np.concatenate([tri, tri, tri], axis=1)
    expand = np.zeros((2 * LANES, SSD_WIDTH), np.float32)
    for h in range(SSD_HEADS):
        expand[h, h * SSD_HEAD_DIM:(h + 1) * SSD_HEAD_DIM] = 1.0
        expand[LANES + h, h * SSD_HEAD_DIM:(h + 1) * SSD_HEAD_DIM] = 1.0
    pair = np.zeros((2, LANES), np.float32)
    pair[0, :SSD_HEAD_DIM] = 1.0
    pair[1, SSD_HEAD_DIM:] = 1.0
    return jnp.asarray(tri3, BF16), jnp.asarray(expand, BF16), jnp.asarray(pair, F32)


def _pad_lanes(v):
    return jnp.pad(v.reshape(1, -1), ((0, 0), (0, DT_PAD - v.shape[-1])))


def _s5_tables(lam_re, lam_im, b_re, b_im, c_re, c_im, log_step):
    step = jnp.exp(log_step.astype(F32))[:, None]
    lam_re = lam_re.astype(F32)
    lam_im = lam_im.astype(F32)
    mag = jnp.exp(lam_re * step)
    ang = lam_im * step
    lb_re = mag * jnp.cos(ang)
    lb_im = mag * jnp.sin(ang)
    den = lam_re * lam_re + lam_im * lam_im
    f_re = ((lb_re - 1.0) * lam_re + lb_im * lam_im) / den
    f_im = (lb_im * lam_re - (lb_re - 1.0) * lam_im) / den
    bb_re = f_re[..., None] * b_re - f_im[..., None] * b_im
    bb_im = f_re[..., None] * b_im + f_im[..., None] * b_re
    eye = jnp.eye(S5_HALF_GROUPS, dtype=F32)

    def in_block(bb):
        bb = bb.reshape(2, S5_HALF_GROUPS, S5_STATE, S5_GROUP_CH)
        return jnp.einsum("agpc,gh->agchp", bb, eye).reshape(2, S5_HALF_CH, S5_HALF_STATE)

    def out_block(cc):
        cc = cc.reshape(2, S5_HALF_GROUPS, S5_GROUP_CH, S5_STATE)
        return jnp.einsum("agcp,gh->agphc", cc, eye).reshape(2, S5_HALF_STATE, S5_HALF_CH)

    bmat = jnp.concatenate([in_block(bb_re), in_block(bb_im)], axis=2).astype(BF16)
    cmat = jnp.concatenate([out_block(c_re), out_block(-c_im)], axis=1).astype(BF16)
    k = jnp.arange(1, S5_SUB + 1, dtype=F32)[:, None, None]
    la = (lam_re * step)[None]
    an = ang[None]
    dfwd_re = (jnp.exp(k * la) * jnp.cos(k * an)).reshape(S5_SUB, S5_PLANE)
    dfwd_im = (jnp.exp(k * la) * jnp.sin(k * an)).reshape(S5_SUB, S5_PLANE)
    dinv_re = (jnp.exp(-k * la) * jnp.cos(k * an)).reshape(S5_SUB, S5_PLANE)
    dinv_im = (-jnp.exp(-k * la) * jnp.sin(k * an)).reshape(S5_SUB, S5_PLANE)
    return bmat, cmat, dinv_re, dinv_im, dfwd_re, dfwd_im


def _ret_constants(l):
    pos = jnp.arange(l, dtype=F32)
    inv_freq = ROPE_BASE ** (-jnp.arange(0, RET_KEY_DIM, 2, dtype=F32) / RET_KEY_DIM)
    ang = pos[:, None] * inv_freq[None, :]
    cos, sin = jnp.cos(ang), jnp.sin(ang)
    cos_t = jnp.tile(jnp.concatenate([cos, cos], axis=1), (1, RET_HEADS))
    sin_t = jnp.tile(jnp.concatenate([-sin, sin], axis=1), (1, RET_HEADS))
    log_g = jnp.log1p(-jnp.exp2(-5.0 - jnp.arange(RET_HEADS, dtype=F32)))
    idx = jnp.arange(CHUNK, dtype=F32)
    diff = idx[:, None] - idx[None, :]
    dmat = jnp.where(diff >= 0, jnp.exp(jnp.maximum(diff, 0.0) * log_g[:, None, None]), 0.0)
    qdec = jnp.repeat(jnp.exp((idx + 1.0)[:, None] * log_g[None, :]), RET_KEY_DIM, axis=1)
    kdect = jnp.repeat(jnp.exp((CHUNK - 1.0 - idx)[:, None] * log_g[None, :]), RET_KEY_DIM, axis=1).T
    cdec = jnp.repeat(jnp.exp(CHUNK * log_g), RET_VAL_DIM).reshape(1, RET_WIDTH)
    headmask = np.kron(np.eye(RET_HEADS, dtype=np.float32), np.ones((1, RET_KEY_DIM), np.float32))
    kvmask = np.kron(np.eye(RET_HEADS, dtype=np.float32), np.ones((RET_KEY_DIM, RET_VAL_DIM), np.float32))
    seg = np.kron(np.eye(RET_HEADS, dtype=np.float32),
                  np.full((RET_VAL_DIM, RET_VAL_DIM), 1.0 / RET_VAL_DIM, np.float32))
    return (cos_t, sin_t, qdec, kdect, dmat, jnp.asarray(headmask, BF16), jnp.asarray(kvmask, F32), cdec,
            jnp.asarray(seg, BF16))


def _pack_w_in(w):
    o = SSD_WIDTH + SSD_CONV_CH
    n_dt = SSD_HEADS
    return jnp.concatenate([w[:, :o + n_dt], jnp.zeros((w.shape[0], DT_PAD - n_dt), w.dtype), w[:, o + n_dt:]],
                           axis=1).astype(BF16)


def _const_spec(shape):
    nd = len(shape)
    return pl.BlockSpec(shape, lambda *_: (0,) * nd, pipeline_mode=pl.Buffered(1))


def _layer(x, final, norm_w, w_in, ssd_params, s5_params, ret_params, cos_t, sin_t, w_out, final_norm_w):
    b, l, d = x.shape
    consts = [norm_w.reshape(1, d), _pack_w_in(w_in)]
    tail = list(ssd_params) + list(s5_params) + list(ret_params) + [w_out.astype(BF16), final_norm_w.reshape(1, d)]
    nc = l // CHUNK
    n = b * nc

    def chunk_spec(lag):
        def index(s):
            t = jnp.clip(s - lag, 0, n - 1)
            return (t // nc, t % nc, 0)
        return pl.BlockSpec((None, CHUNK, d), index)

    pos_spec = pl.BlockSpec((CHUNK, RET_QK), lambda s: (jnp.clip(s - 1, 0, n - 1) % nc, 0))
    return pl.pallas_call(
        functools.partial(_layer_kernel, final, nc),
        grid=(n + 2,),
        in_specs=[chunk_spec(0), chunk_spec(2)] + [_const_spec(p.shape) for p in consts] + [pos_spec, pos_spec]
        + [_const_spec(p.shape) for p in tail],
        out_specs=chunk_spec(2),
        out_shape=jax.ShapeDtypeStruct((b, l, d), F32),
        scratch_shapes=[pltpu.VMEM((2, CHUNK, w), F32) for _, w in PIECES]
        + [pltpu.VMEM((2, CHUNK, MIX_WIDTH), BF16),
           pltpu.VMEM((CHUNK, D_MODEL), BF16),
           pltpu.VMEM((SUBLANES, SSD_CONV_CH), F32),
           pltpu.VMEM((SSD_STATE, SSD_WIDTH), F32),
           pltpu.VMEM((1, S5_PLANE), F32), pltpu.VMEM((1, S5_PLANE), F32),
           pltpu.VMEM((CHUNK, 2 * S5_PLANE), BF16),
           pltpu.VMEM((RET_QK, RET_WIDTH), F32)],
        compiler_params=pltpu.CompilerParams(dimension_semantics=("arbitrary",), vmem_limit_bytes=VMEM_LIMIT),
        name="layer",
    )(x, x, *consts, cos_t, sin_t, *tail)


def kernel(x, norm_w, w_in, conv_w, conv_b, dt_bias, a_log, d_ssd, ssd_norm_w, s5_lambda_re, s5_lambda_im,
           s5_b_re, s5_b_im, s5_c_re, s5_c_im, s5_d, s5_log_step, s5_w_glu, s5_b_glu, ret_norm_w, w_out,
           final_norm_w):
    b, l, d = x.shape
    depth = w_in.shape[0]
    tri3, expand, pair = _ssd_constants()
    cos_t, sin_t, qdec, kdect, dmat, headmask, kvmask, cdec, seg = _ret_constants(l)
    s5tri = jnp.asarray(np.kron(np.eye(CHUNK // S5_SUB, dtype=np.float32),
                                np.tril(np.ones((S5_SUB, S5_SUB), np.float32))), BF16)
    h_res = x.astype(F32)
    for i in range(depth):
        ssd_params = [conv_w[i], conv_b[i].reshape(1, -1), _pad_lanes(dt_bias[i]), _pad_lanes(a_log[i]),
                      jnp.repeat(d_ssd[i], SSD_HEAD_DIM).reshape(1, SSD_WIDTH), ssd_norm_w[i].reshape(1, -1),
                      tri3, expand, pair]
        bmat, cmat, dinv_re, dinv_im, dfwd_re, dfwd_im = _s5_tables(
            s5_lambda_re[i], s5_lambda_im[i], s5_b_re[i], s5_b_im[i], s5_c_re[i], s5_c_im[i], s5_log_step[i])
        s5_params = [bmat, cmat, dinv_re, dinv_im, dfwd_re, dfwd_im, s5tri, s5_d[i].reshape(1, -1),
                     s5_w_glu[i].astype(BF16), s5_b_glu[i].reshape(1, -1)]
        ret_params = [qdec, kdect, dmat, headmask, kvmask, cdec, seg, ret_norm_w[i].reshape(1, -1)]
        h_res = _layer(h_res, i == depth - 1, norm_w[i], w_in[i], ssd_params, s5_params, ret_params, cos_t, sin_t,
                       w_out[i], final_norm_w)
    return h_res.astype(x.dtype)
```

```python
import functools
import math

import numpy as np
import jax
import jax.numpy as jnp
from jax import lax
from jax.experimental import pallas as pl
from jax.experimental.pallas import tpu as pltpu

F32 = jnp.float32
BF16 = jnp.bfloat16

D_MODEL = 1024
CHUNK = 128
EPS = 1e-6
SSD_HEADS = 16
SSD_HEAD_DIM = 64
SSD_WIDTH = SSD_HEADS * SSD_HEAD_DIM
SSD_GROUPS = 2
SSD_STATE = 128
CONV_WIDTH = 4
SSD_BC = SSD_GROUPS * SSD_STATE
SSD_CONV_CH = SSD_WIDTH + 2 * SSD_BC
S5_GROUP_CH = 16
S5_GROUPS = 32
S5_WIDTH = S5_GROUPS * S5_GROUP_CH
S5_STATE = 64
S5_HALF_GROUPS = S5_GROUPS // 2
S5_HALF_CH = S5_WIDTH // 2
S5_HALF_STATE = S5_HALF_GROUPS * S5_STATE
S5_PLANE = S5_GROUPS * S5_STATE
S5_SUB = 16
RET_HEADS = 8
RET_KEY_DIM = 32
RET_VAL_DIM = 64
RET_QK = RET_HEADS * RET_KEY_DIM
RET_WIDTH = RET_HEADS * RET_VAL_DIM
ROPE_BASE = 10000.0
MIX_WIDTH = SSD_WIDTH + S5_WIDTH + RET_WIDTH
LANES = 128
SUBLANES = 8
DT_PAD = LANES
PIECES = (("z", SSD_WIDTH), ("xbc", SSD_CONV_CH), ("dt", DT_PAD), ("s5g", S5_WIDTH), ("s5u", S5_WIDTH),
          ("q", RET_QK), ("k", RET_QK), ("v", RET_WIDTH), ("rg", RET_WIDTH))
PIECE_OFF = {}
_o = 0
for _n, _w in PIECES:
    PIECE_OFF[_n] = (_o, _w)
    _o += _w
IN_PACKED = _o
VMEM_LIMIT = 56 * 1024 * 1024


def _silu(x):
    h = 0.5 * x
    return h + h * jnp.tanh(h)


def _sigmoid(x):
    return 0.5 + 0.5 * jnp.tanh(0.5 * x)


def _split3(x):
    p1 = x.astype(BF16)
    r1 = x - p1.astype(F32)
    p2 = r1.astype(BF16)
    p3 = (r1 - p2.astype(F32)).astype(BF16)
    return p1, p2, p3


def _pair_rhs(xp, mask_l, mask_r):
    return jnp.concatenate([(xp * mask_l).astype(BF16), (xp * mask_r).astype(BF16)], axis=0)


def _kept(value, keep):
    return value if keep is None else value * keep


def _ssd_chunk(fill, keep, z, xbc_raw, dt_raw, convw_ref, convb_ref, dtb_ref, alog_ref, dskip_ref, nw_ref,
               tri3_ref, expand_ref, mask_l, mask_r, xprev_ref, state_ref):
    xprev = xprev_ref[...]
    row8 = lax.broadcasted_iota(jnp.int32, (SUBLANES, UNIT_COLS), 0)
    blocks = []
    for c0 in range(0, SSD_CONV_CH, UNIT_COLS):
        cols = slice(c0, c0 + UNIT_COLS)
        xb = xbc_raw[:, cols]
        acc = convb_ref[:, cols] + convw_ref[CONV_WIDTH - 1:CONV_WIDTH, cols] * xb
        for j in range(1, CONV_WIDTH):
            rolled = pltpu.roll(xb, j, 0)
            head = jnp.where(row8 < j, pltpu.roll(xprev[:, cols], j, 0), rolled[0:SUBLANES, :])
            shifted = jnp.concatenate([head, rolled[SUBLANES:, :]], axis=0)
            acc = acc + convw_ref[CONV_WIDTH - 1 - j:CONV_WIDTH - j, cols] * shifted
        blocks.append(_silu(acc))
        if (c0 // UNIT_COLS) % 2 == 1:
            fill(1)
    xprev_ref[...] = _kept(xbc_raw[CHUNK - SUBLANES:, :], keep)
    xbc = jnp.concatenate(blocks, axis=1)
    xs = xbc[:, :SSD_WIDTH]
    bm = xbc[:, SSD_WIDTH:SSD_WIDTH + SSD_BC]
    cm = xbc[:, SSD_WIDTH + SSD_BC:]

    dtr = dt_raw + dtb_ref[...]
    dt = jnp.maximum(dtr, 0.0) + jnp.log1p(jnp.exp(-jnp.abs(dtr)))
    a = -jnp.exp(alog_ref[...])
    dta = dt * a
    acum = jnp.dot(tri3_ref[...], jnp.concatenate(_split3(dta), axis=0), preferred_element_type=F32)
    acum_t = acum.T
    dt_t = dt.T
    ea = jnp.exp(acum)
    wgt = jnp.exp(acum[CHUNK - 1:CHUNK, :] - acum) * dt

    def expand(v):
        hi = v.astype(BF16)
        lo = (v - hi.astype(F32)).astype(BF16)
        return jnp.dot(jnp.concatenate([hi, lo], axis=1), expand_ref[...], preferred_element_type=F32)

    wgt_x = expand(wgt)
    ea_x = expand(ea)

    xw_b = (xs * wgt_x).astype(BF16)
    row = lax.broadcasted_iota(jnp.int32, (CHUNK, CHUNK), 0)
    col = lax.broadcasted_iota(jnp.int32, (CHUNK, CHUNK), 1)
    causal = row >= col
    hg = SSD_HEADS // SSD_GROUPS
    gw = hg * SSD_HEAD_DIM
    y_parts = []
    s_parts = []
    for g in range(SSD_GROUPS):
        bm_g = bm[:, g * SSD_STATE:(g + 1) * SSD_STATE]
        cm_b = cm[:, g * SSD_STATE:(g + 1) * SSD_STATE].astype(BF16)
        bm_b = bm_g.astype(BF16)
        cb = lax.dot_general(cm_b, bm_b, (((1,), (1,)), ((), ())), preferred_element_type=F32)
        s_parts.append(jnp.dot(bm_g.T.astype(BF16), xw_b[:, g * gw:(g + 1) * gw], preferred_element_type=F32))
        y_off = jnp.dot(cm_b, state_ref[:, g * gw:(g + 1) * gw].astype(BF16), preferred_element_type=F32)
        fill(1)
        scores = []
        for r in range(hg):
            h = g * hg + r
            seg = acum[:, h:h + 1] - acum_t[h:h + 1, :]
            decay = jnp.exp(jnp.where(causal, seg, -jnp.inf))
            scores.append((cb * decay * dt_t[h:h + 1, :]).astype(BF16))
            if r % 2 == 1:
                j = r // 2
                lo_ = g * gw + j * LANES
                rhs = _pair_rhs(xs[:, lo_:lo_ + LANES], mask_l, mask_r)
                lhs = jnp.concatenate([scores[2 * j], scores[2 * j + 1]], axis=1)
                y_parts.append(jnp.dot(lhs, rhs, preferred_element_type=F32)
                               + y_off[:, j * LANES:(j + 1) * LANES] * ea_x[:, lo_:lo_ + LANES])
                fill(1)
    state_ref[...] = _kept(state_ref[...] * ea_x[CHUNK - 1:CHUNK, :] + jnp.concatenate(s_parts, axis=1), keep)

    outs = []
    for g in range(SSD_GROUPS):
        cols = slice(g * gw, (g + 1) * gw)
        yg = jnp.concatenate(y_parts[g * (hg // 2):(g + 1) * (hg // 2)], axis=1) + xs[:, cols] * dskip_ref[:, cols]
        yg = yg * _silu(z[:, cols])
        ms = jnp.mean(yg * yg, axis=-1, keepdims=True)
        outs.append(yg * lax.rsqrt(ms + EPS) * nw_ref[:, cols])
        fill(1)
    return jnp.concatenate(outs, axis=1).astype(BF16)


def _s5_chunk(fill, keep, u, gate, bmat_ref, cmat_ref, dinv_re_ref, dinv_im_ref, dfwd_re_ref, dfwd_im_ref,
              tri_ref, dskip_ref, wglu_ref, bglu_ref, carry_re_ref, carry_im_ref, x_ref):
    u_b = u.astype(BF16)
    bu_re, bu_im = [], []
    for a in range(2):
        bu = jnp.dot(u_b[:, a * S5_HALF_CH:(a + 1) * S5_HALF_CH], bmat_ref[a], preferred_element_type=F32)
        bu_re.append(bu[:, :S5_HALF_STATE])
        bu_im.append(bu[:, S5_HALF_STATE:])
    bu_re = jnp.concatenate(bu_re, axis=1)
    bu_im = jnp.concatenate(bu_im, axis=1)
    nsub = CHUNK // S5_SUB
    di_re = jnp.concatenate([dinv_re_ref[...]] * nsub, axis=0)
    di_im = jnp.concatenate([dinv_im_ref[...]] * nsub, axis=0)
    bs_re = (bu_re * di_re - bu_im * di_im).astype(BF16)
    bs_im = (bu_re * di_im + bu_im * di_re).astype(BF16)
    fill(1)
    z_re = jnp.dot(tri_ref[...], bs_re, preferred_element_type=F32)
    z_im = jnp.dot(tri_ref[...], bs_im, preferred_element_type=F32)
    cr = carry_re_ref[...]
    ci = carry_im_ref[...]
    df_re = dfwd_re_ref[...]
    df_im = dfwd_im_ref[...]
    for m in range(nsub):
        zr = z_re[m * S5_SUB:(m + 1) * S5_SUB, :] + cr
        zi = z_im[m * S5_SUB:(m + 1) * S5_SUB, :] + ci
        xr = zr * df_re - zi * df_im
        xi = zr * df_im + zi * df_re
        cr = xr[S5_SUB - 1:S5_SUB, :]
        ci = xi[S5_SUB - 1:S5_SUB, :]
        x_ref[m * S5_SUB:(m + 1) * S5_SUB, :S5_PLANE] = xr.astype(BF16)
        x_ref[m * S5_SUB:(m + 1) * S5_SUB, S5_PLANE:] = xi.astype(BF16)
        if m % 4 == 3:
            fill(1)
    carry_re_ref[...] = _kept(cr, keep)
    carry_im_ref[...] = _kept(ci, keep)
    ys = []
    for a in range(2):
        xa = jnp.concatenate([x_ref[:, a * S5_HALF_STATE:(a + 1) * S5_HALF_STATE],
                              x_ref[:, S5_PLANE + a * S5_HALF_STATE:S5_PLANE + (a + 1) * S5_HALF_STATE]], axis=1)
        ys.append(jnp.dot(xa, cmat_ref[a], preferred_element_type=F32))
    y = jnp.concatenate(ys, axis=1) + dskip_ref[...] * u
    y = 0.5 * y * (1.0 + jnp.tanh(math.sqrt(2.0 / math.pi) * (y + 0.044715 * (y * y * y))))
    glu = jnp.dot(y.astype(BF16), wglu_ref[...], preferred_element_type=F32) + bglu_ref[...]
    return (y * _sigmoid(glu) * _silu(gate)).astype(BF16)


def _ret_chunk(fill, keep, q_raw, k_raw, v, gate, cos_ref, sin_ref, qdec_ref, kdect_ref, dmat_ref, headmask_ref,
               kvmask_ref, cdec_ref, mask_l, mask_r, seg_ref, nw_ref, state_ref):
    half = RET_KEY_DIM // 2
    lane = lax.broadcasted_iota(jnp.int32, (CHUNK, RET_QK), 1)
    first_half = (lane % RET_KEY_DIM) < half
    cos = cos_ref[...]
    sin = sin_ref[...]

    def rotary(t):
        swapped = jnp.where(first_half, pltpu.roll(t, RET_QK - half, 1), pltpu.roll(t, half, 1))
        return t * cos + swapped * sin

    q = rotary(q_raw)
    fill(1)
    k = rotary(k_raw) * (RET_KEY_DIM ** -0.5)
    fill(1)
    v_b = v.astype(BF16)
    k_t = k.T
    q_b = q.astype(BF16)
    q_heads = jnp.concatenate([q_b * headmask_ref[h:h + 1, :] for h in range(RET_HEADS)], axis=0)
    scores = jnp.dot(q_heads, k_t.astype(BF16), preferred_element_type=F32)
    y_cross = jnp.dot((q * qdec_ref[...]).astype(BF16), state_ref[...].astype(BF16), preferred_element_type=F32)
    y_parts = []
    for j in range(RET_HEADS // 2):
        sa = (scores[(2 * j) * CHUNK:(2 * j + 1) * CHUNK, :] * dmat_ref[2 * j]).astype(BF16)
        sb = (scores[(2 * j + 1) * CHUNK:(2 * j + 2) * CHUNK, :] * dmat_ref[2 * j + 1]).astype(BF16)
        rhs = _pair_rhs(v[:, j * LANES:(j + 1) * LANES], mask_l, mask_r)
        y_parts.append(jnp.dot(jnp.concatenate([sa, sb], axis=1), rhs, preferred_element_type=F32))
        if j % 2 == 1:
            fill(1)
    y = jnp.concatenate(y_parts, axis=1) + y_cross
    kv = jnp.dot((k_t * kdect_ref[...]).astype(BF16), v_b, preferred_element_type=F32)
    state_ref[...] = _kept(state_ref[...] * cdec_ref[...] + kv * kvmask_ref[...], keep)
    ms = jnp.dot((y * y).astype(BF16), seg_ref[...], preferred_element_type=F32)
    return (y * lax.rsqrt(ms + EPS) * nw_ref[...] * _silu(gate)).astype(BF16)


N_SSD, N_S5, N_RET = 9, 10, 8


class _Filler:
    def __init__(self):
        self.units = []

    def add(self, fn):
        self.units.append(fn)

    def __call__(self, n=1):
        for _ in range(min(n, len(self.units))):
            self.units.pop(0)()


UNIT_COLS = 256


def _layer_kernel(final, nc, xin_ref, xres_ref, nw_ref, win_ref, cos_a_ref, sin_a_ref, cos_b_ref, sin_b_ref, *rest):
    ssd_p = rest[:N_SSD]
    s5_p = rest[N_SSD:N_SSD + N_S5]
    ret_p = rest[N_SSD + N_S5:N_SSD + N_S5 + N_RET]
    wout_ref, fnw_ref, out_ref = rest[N_SSD + N_S5 + N_RET:N_SSD + N_S5 + N_RET + 3]
    scratch = rest[N_SSD + N_S5 + N_RET + 3:]
    proj = dict(zip([n for n, _ in PIECES], scratch[:len(PIECES)]))
    y_ref, h_ref, xprev_ref, ssd_state_ref, carry_re_ref, carry_im_ref, s5x_ref, ret_state_ref = scratch[len(PIECES):]
    s = pl.program_id(0)

    @pl.when(s == 0)
    def _():
        for ref in list(proj.values()) + [y_ref, xprev_ref, ssd_state_ref, carry_re_ref, carry_im_ref,
                                          ret_state_ref]:
            ref[...] = jnp.zeros_like(ref)

    convw, convb, dtb, alog, dskip, ssd_nw, tri3, expand, pairmask = ssd_p
    qdec, kdect, dmat, headmask, kvmask, cdec, seg, ret_nw = ret_p
    mask_l = pairmask[0:1, :]
    mask_r = pairmask[1:2, :]

    def half_step(rows, wr, rd, cos_ref, sin_ref, keep):
        fill = _Filler()

        def out_unit(c0):
            def run():
                out_ref[rows, c0:c0 + UNIT_COLS] = xres_ref[rows, c0:c0 + UNIT_COLS] + jnp.dot(
                    y_ref[wr], wout_ref[:, c0:c0 + UNIT_COLS], preferred_element_type=F32)
            return run

        def in_unit(name, c0, c1):
            off, _ = PIECE_OFF[name]

            def run():
                proj[name][wr, :, c0:c1] = jnp.dot(h_ref[wr], win_ref[:, off + c0:off + c1],
                                                   preferred_element_type=F32)
            return run

        for c0 in range(0, D_MODEL, UNIT_COLS):
            fill.add(out_unit(c0))
        fill(1)
        x = xin_ref[rows, :]
        r = lax.rsqrt(jnp.mean(x * x, axis=-1, keepdims=True) + EPS)
        h_ref[wr] = (x * r * nw_ref[...]).astype(BF16)
        for name, width in PIECES:
            for c0 in range(0, width, UNIT_COLS):
                fill.add(in_unit(name, c0, min(c0 + UNIT_COLS, width)))

        y_ref[rd, :, 0:SSD_WIDTH] = _ssd_chunk(fill, keep, proj["z"][rd], proj["xbc"][rd], proj["dt"][rd], convw,
                                               convb, dtb, alog, dskip, ssd_nw, tri3, expand, mask_l, mask_r,
                                               xprev_ref, ssd_state_ref)
        y_ref[rd, :, SSD_WIDTH + S5_WIDTH:] = _ret_chunk(fill, keep, proj["q"][rd], proj["k"][rd], proj["v"][rd],
                                                         proj["rg"][rd], cos_ref, sin_ref, qdec, kdect, dmat,
                                                         headmask, kvmask, cdec, mask_l, mask_r, seg, ret_nw,
                                                         ret_state_ref)
        y_ref[rd, :, SSD_WIDTH:SSD_WIDTH + S5_WIDTH] = _s5_chunk(fill, keep, proj["s5u"][rd], proj["s5g"][rd],
                                                                 *s5_p, carry_re_ref, carry_im_ref, s5x_ref)
        fill(len(fill.units))

    keep_a = jnp.where((2 * s) % nc == 0, 0.0, 1.0).astype(F32)
    half_step(slice(0, CHUNK), 0, 1, cos_a_ref, sin_a_ref, keep_a)
    half_step(slice(CHUNK, 2 * CHUNK), 1, 0, cos_b_ref, sin_b_ref, None)
    if final:
        acc = out_ref[...]
        rr = lax.rsqrt(jnp.mean(acc * acc, axis=-1, keepdims=True) + EPS)
        out_ref[...] = acc * rr * fnw_ref[...]


def _ssd_constants():
    tri = np.tril(np.ones((CHUNK, CHUNK), np.float32))
    tri3 = np.concatenate([tri, tri, tri], axis=1)
    expand = np.zeros((2 * LANES, SSD_WIDTH), np.float32)
    for h in range(SSD_HEADS):
        expand[h, h * SSD_HEAD_DIM:(h + 1) * SSD_HEAD_DIM] = 1.0
        expand[LANES + h, h * SSD_HEAD_DIM:(h + 1) * SSD_HEAD_DIM] = 1.0
    pair = np.zeros((2, LANES), np.float32)
    pair[0, :SSD_HEAD_DIM] = 1.0
    pair[1, SSD_HEAD_DIM:] = 1.0
    return jnp.asarray(tri3, BF16), jnp.asarray(expand, BF16), jnp.asarray(pair, F32)


def _pad_lanes(v):
    return jnp.pad(v.reshape(1, -1), ((0, 0), (0, DT_PAD - v.shape[-1])))


def _s5_tables(lam_re, lam_im, b_re, b_im, c_re, c_im, log_step):
    step = jnp.exp(log_step.astype(F32))[:, None]
    lam_re = lam_re.astype(F32)
    lam_im = lam_im.astype(F32)
    mag = jnp.exp(lam_re * step)
    ang = lam_im * step
    lb_re = mag * jnp.cos(ang)
    lb_im = mag * jnp.sin(ang)
    den = lam_re * lam_re + lam_im * lam_im
    f_re = ((lb_re - 1.0) * lam_re + lb_im * lam_im) / den
    f_im = (lb_im * lam_re - (lb_re - 1.0) * lam_im) / den
    bb_re = f_re[..., None] * b_re - f_im[..., None] * b_im
    bb_im = f_re[..., None] * b_im + f_im[..., None] * b_re
    eye = jnp.eye(S5_HALF_GROUPS, dtype=F32)

    def in_block(bb):
        bb = bb.reshape(2, S5_HALF_GROUPS, S5_STATE, S5_GROUP_CH)
        return jnp.einsum("agpc,gh->agchp", bb, eye).reshape(2, S5_HALF_CH, S5_HALF_STATE)

    def out_block(cc):
        cc = cc.reshape(2, S5_HALF_GROUPS, S5_GROUP_CH, S5_STATE)
        return jnp.einsum("agcp,gh->agphc", cc, eye).reshape(2, S5_HALF_STATE, S5_HALF_CH)

    bmat = jnp.concatenate([in_block(bb_re), in_block(bb_im)], axis=2).astype(BF16)
    cmat = jnp.concatenate([out_block(c_re), out_block(-c_im)], axis=1).astype(BF16)
    k = jnp.arange(1, S5_SUB + 1, dtype=F32)[:, None, None]
    la = (lam_re * step)[None]
    an = ang[None]
    dfwd_re = (jnp.exp(k * la) * jnp.cos(k * an)).reshape(S5_SUB, S5_PLANE)
    dfwd_im = (jnp.exp(k * la) * jnp.sin(k * an)).reshape(S5_SUB, S5_PLANE)
    dinv_re = (jnp.exp(-k * la) * jnp.cos(k * an)).reshape(S5_SUB, S5_PLANE)
    dinv_im = (-jnp.exp(-k * la) * jnp.sin(k * an)).reshape(S5_SUB, S5_PLANE)
    return bmat, cmat, dinv_re, dinv_im, dfwd_re, dfwd_im


def _ret_constants(l):
    pos = jnp.arange(l, dtype=F32)
    inv_freq = ROPE_BASE ** (-jnp.arange(0, RET_KEY_DIM, 2, dtype=F32) / RET_KEY_DIM)
    ang = pos[:, None] * inv_freq[None, :]
    cos, sin = jnp.cos(ang), jnp.sin(ang)
    cos_t = jnp.tile(jnp.concatenate([cos, cos], axis=1), (1, RET_HEADS))
    sin_t = jnp.tile(jnp.concatenate([-sin, sin], axis=1), (1, RET_HEADS))
    log_g = jnp.log1p(-jnp.exp2(-5.0 - jnp.arange(RET_HEADS, dtype=F32)))
    idx = jnp.arange(CHUNK, dtype=F32)
    diff = idx[:, None] - idx[None, :]
    dmat = jnp.where(diff >= 0, jnp.exp(jnp.maximum(diff, 0.0) * log_g[:, None, None]), 0.0)
    qdec = jnp.repeat(jnp.exp((idx + 1.0)[:, None] * log_g[None, :]), RET_KEY_DIM, axis=1)
    kdect = jnp.repeat(jnp.exp((CHUNK - 1.0 - idx)[:, None] * log_g[None, :]), RET_KEY_DIM, axis=1).T
    cdec = jnp.repeat(jnp.exp(CHUNK * log_g), RET_VAL_DIM).reshape(1, RET_WIDTH)
    headmask = np.kron(np.eye(RET_HEADS, dtype=np.float32), np.ones((1, RET_KEY_DIM), np.float32))
    kvmask = np.kron(np.eye(RET_HEADS, dtype=np.float32), np.ones((RET_KEY_DIM, RET_VAL_DIM), np.float32))
    seg = np.kron(np.eye(RET_HEADS, dtype=np.float32),
                  np.full((RET_VAL_DIM, RET_VAL_DIM), 1.0 / RET_VAL_DIM, np.float32))
    return (cos_t, sin_t, qdec, kdect, dmat, jnp.asarray(headmask, BF16), jnp.asarray(kvmask, F32), cdec,
            jnp.asarray(seg, BF16))


def _pack_w_in(w):
    o = SSD_WIDTH + SSD_CONV_CH
    n_dt = SSD_HEADS
    w = w.astype(BF16)
    return jnp.concatenate([w[:, :o + n_dt], jnp.zeros((w.shape[0], DT_PAD - n_dt), BF16), w[:, o + n_dt:]], axis=1)


def _const_spec(shape):
    nd = len(shape)
    return pl.BlockSpec(shape, lambda *_: (0,) * nd, pipeline_mode=pl.Buffered(1))


def _layer(x, final, norm_w, w_in, ssd_params, s5_params, ret_params, cos_t, sin_t, w_out, final_norm_w):
    b, l, d = x.shape
    consts = [norm_w.reshape(1, d), _pack_w_in(w_in)]
    tail = list(ssd_params) + list(s5_params) + list(ret_params) + [w_out.astype(BF16), final_norm_w.reshape(1, d)]
    nc = l // CHUNK
    assert nc % 2 == 0, "sequences must hold an even number of chunks"
    nb = b * nc // 2

    def block_spec(lag):
        def index(s):
            t = jnp.clip(s - lag, 0, nb - 1)
            return (t // (nc // 2), t % (nc // 2), 0)
        return pl.BlockSpec((None, 2 * CHUNK, d), index)

    def pos_spec(lag):
        return pl.BlockSpec((CHUNK, RET_QK), lambda s: (jnp.clip(2 * s - lag, 0, 2 * nb - 1) % nc, 0))

    return pl.pallas_call(
        functools.partial(_layer_kernel, final, nc),
        grid=(nb + 1,),
        in_specs=[block_spec(0), block_spec(1)] + [_const_spec(p.shape) for p in consts]
        + [pos_spec(1), pos_spec(1), pos_spec(0), pos_spec(0)] + [_const_spec(p.shape) for p in tail],
        out_specs=block_spec(1),
        out_shape=jax.ShapeDtypeStruct((b, l, d), F32),
        scratch_shapes=[pltpu.VMEM((2, CHUNK, w), F32) for _, w in PIECES]
        + [pltpu.VMEM((2, CHUNK, MIX_WIDTH), BF16),
           pltpu.VMEM((2, CHUNK, D_MODEL), BF16),
           pltpu.VMEM((SUBLANES, SSD_CONV_CH), F32),
           pltpu.VMEM((SSD_STATE, SSD_WIDTH), F32),
           pltpu.VMEM((1, S5_PLANE), F32), pltpu.VMEM((1, S5_PLANE), F32),
           pltpu.VMEM((CHUNK, 2 * S5_PLANE), BF16),
           pltpu.VMEM((RET_QK, RET_WIDTH), F32)],
        compiler_params=pltpu.CompilerParams(dimension_semantics=("arbitrary",), vmem_limit_bytes=VMEM_LIMIT),
        name="layer",
    )(x, x, *consts, cos_t, sin_t, cos_t, sin_t, *tail)


def kernel(x, norm_w, w_in, conv_w, conv_b, dt_bias, a_log, d_ssd, ssd_norm_w, s5_lambda_re, s5_lambda_im,
           s5_b_re, s5_b_im, s5_c_re, s5_c_im, s5_d, s5_log_step, s5_w_glu, s5_b_glu, ret_norm_w, w_out,
           final_norm_w):
    b, l, d = x.shape
    depth = w_in.shape[0]
    tri3, expand, pair = _ssd_constants()
    cos_t, sin_t, qdec, kdect, dmat, headmask, kvmask, cdec, seg = _ret_constants(l)
    s5tri = jnp.asarray(np.kron(np.eye(CHUNK // S5_SUB, dtype=np.float32),
                                np.tril(np.ones((S5_SUB, S5_SUB), np.float32))), BF16)
    h_res = x.astype(F32)
    for i in range(depth):
        ssd_params = [conv_w[i], conv_b[i].reshape(1, -1), _pad_lanes(dt_bias[i]), _pad_lanes(a_log[i]),
                      jnp.repeat(d_ssd[i], SSD_HEAD_DIM).reshape(1, SSD_WIDTH), ssd_norm_w[i].reshape(1, -1),
                      tri3, expand, pair]
        bmat, cmat, dinv_re, dinv_im, dfwd_re, dfwd_im = _s5_tables(
            s5_lambda_re[i], s5_lambda_im[i], s5_b_re[i], s5_b_im[i], s5_c_re[i], s5_c_im[i], s5_log_step[i])
        s5_params = [bmat, cmat, dinv_re, dinv_im, dfwd_re, dfwd_im, s5tri, s5_d[i].reshape(1, -1),
                     s5_w_glu[i].astype(BF16), s5_b_glu[i].reshape(1, -1)]
        ret_params = [qdec, kdect, dmat, headmask, kvmask, cdec, seg, ret_norm_w[i].reshape(1, -1)]
        h_res = _layer(h_res, i == depth - 1, norm_w[i], w_in[i], ssd_params, s5_params, ret_params, cos_t, sin_t,
                       w_out[i], final_norm_w)
    return h_res.astype(x.dtype)
```

```python
import functools
import math

import numpy as np
import jax
import jax.numpy as jnp
from jax import lax
from jax.experimental import pallas as pl
from jax.experimental.pallas import tpu as pltpu

F32 = jnp.float32
BF16 = jnp.bfloat16

D_MODEL = 1024
CHUNK = 128
EPS = 1e-6
SSD_HEADS = 16
SSD_HEAD_DIM = 64
SSD_WIDTH = SSD_HEADS * SSD_HEAD_DIM
SSD_GROUPS = 2
SSD_STATE = 128
CONV_WIDTH = 4
SSD_BC = SSD_GROUPS * SSD_STATE
SSD_CONV_CH = SSD_WIDTH + 2 * SSD_BC
S5_GROUP_CH = 16
S5_GROUPS = 32
S5_WIDTH = S5_GROUPS * S5_GROUP_CH
S5_STATE = 64
S5_HALF_GROUPS = S5_GROUPS // 2
S5_HALF_CH = S5_WIDTH // 2
S5_HALF_STATE = S5_HALF_GROUPS * S5_STATE
S5_PLANE = S5_GROUPS * S5_STATE
S5_SUB = 16
RET_HEADS = 8
RET_KEY_DIM = 32
RET_VAL_DIM = 64
RET_QK = RET_HEADS * RET_KEY_DIM
RET_WIDTH = RET_HEADS * RET_VAL_DIM
ROPE_BASE = 10000.0
MIX_WIDTH = SSD_WIDTH + S5_WIDTH + RET_WIDTH
LANES = 128
SUBLANES = 8
DT_PAD = LANES
PIECES = (("z", SSD_WIDTH), ("xbc", SSD_CONV_CH), ("dt", DT_PAD), ("s5g", S5_WIDTH), ("s5u", S5_WIDTH),
          ("q", RET_QK), ("k", RET_QK), ("v", RET_WIDTH), ("rg", RET_WIDTH))
PIECE_OFF = {}
_o = 0
for _n, _w in PIECES:
    PIECE_OFF[_n] = (_o, _w)
    _o += _w
IN_PACKED = _o
VMEM_LIMIT = 56 * 1024 * 1024


def _silu(x):
    h = 0.5 * x
    return h + h * jnp.tanh(h)


def _sigmoid(x):
    return 0.5 + 0.5 * jnp.tanh(0.5 * x)


def _split3(x):
    p1 = x.astype(BF16)
    r1 = x - p1.astype(F32)
    p2 = r1.astype(BF16)
    p3 = (r1 - p2.astype(F32)).astype(BF16)
    return p1, p2, p3


def _pair_rhs(xp, mask_l, mask_r):
    return jnp.concatenate([(xp * mask_l).astype(BF16), (xp * mask_r).astype(BF16)], axis=0)


def _kept(value, keep):
    return value if keep is None else value * keep


def _ssd_chunk(emit, keep, z, xbc_raw, dt_raw, convw_ref, convb_ref, dtb_ref, alog_ref, dskip_ref, nw_ref,
               tri3_ref, expand_ref, mask_l, mask_r, xprev_ref, state_ref):
    xprev = xprev_ref[...]
    row8 = lax.broadcasted_iota(jnp.int32, (SUBLANES, UNIT_COLS), 0)
    blocks = []
    for c0 in range(0, SSD_CONV_CH, UNIT_COLS):
        cols = slice(c0, c0 + UNIT_COLS)
        xb = xbc_raw[:, cols]
        acc = convb_ref[:, cols] + convw_ref[CONV_WIDTH - 1:CONV_WIDTH, cols] * xb
        for j in range(1, CONV_WIDTH):
            rolled = pltpu.roll(xb, j, 0)
            head = jnp.where(row8 < j, pltpu.roll(xprev[:, cols], j, 0), rolled[0:SUBLANES, :])
            shifted = jnp.concatenate([head, rolled[SUBLANES:, :]], axis=0)
            acc = acc + convw_ref[CONV_WIDTH - 1 - j:CONV_WIDTH - j, cols] * shifted
        blocks.append(_silu(acc))
        yield
    xprev_ref[...] = _kept(xbc_raw[CHUNK - SUBLANES:, :], keep)
    xbc = jnp.concatenate(blocks, axis=1)
    xs = xbc[:, :SSD_WIDTH]
    bm = xbc[:, SSD_WIDTH:SSD_WIDTH + SSD_BC]
    cm = xbc[:, SSD_WIDTH + SSD_BC:]

    dtr = dt_raw + dtb_ref[...]
    dt = jnp.maximum(dtr, 0.0) + jnp.log1p(jnp.exp(-jnp.abs(dtr)))
    a = -jnp.exp(alog_ref[...])
    dta3 = jnp.concatenate(_split3(dt * a), axis=0)
    yield
    acum = jnp.dot(tri3_ref[...], dta3, preferred_element_type=F32)
    acum_t = acum.T
    dt_t = dt.T
    ea = jnp.exp(acum)
    wgt = jnp.exp(acum[CHUNK - 1:CHUNK, :] - acum) * dt

    def hi_lo(v):
        hi = v.astype(BF16)
        lo = (v - hi.astype(F32)).astype(BF16)
        return jnp.concatenate([hi, lo], axis=1)

    wgt_hl = hi_lo(wgt)
    ea_hl = hi_lo(ea)
    yield
    wgt_x = jnp.dot(wgt_hl, expand_ref[...], preferred_element_type=F32)
    ea_x = jnp.dot(ea_hl, expand_ref[...], preferred_element_type=F32)

    xw_b = (xs * wgt_x).astype(BF16)
    row = lax.broadcasted_iota(jnp.int32, (CHUNK, CHUNK), 0)
    col = lax.broadcasted_iota(jnp.int32, (CHUNK, CHUNK), 1)
    causal = row >= col
    hg = SSD_HEADS // SSD_GROUPS
    gw = hg * SSD_HEAD_DIM
    y_parts = []
    s_parts = []
    for g in range(SSD_GROUPS):
        bm_g = bm[:, g * SSD_STATE:(g + 1) * SSD_STATE]
        cm_b = cm[:, g * SSD_STATE:(g + 1) * SSD_STATE].astype(BF16)
        bm_b = bm_g.astype(BF16)
        bm_tb = bm_g.T.astype(BF16)
        state_b = state_ref[:, g * gw:(g + 1) * gw].astype(BF16)
        yield
        cb = lax.dot_general(cm_b, bm_b, (((1,), (1,)), ((), ())), preferred_element_type=F32)
        s_parts.append(jnp.dot(bm_tb, xw_b[:, g * gw:(g + 1) * gw], preferred_element_type=F32))
        y_off = jnp.dot(cm_b, state_b, preferred_element_type=F32)
        scores = []
        for r in range(hg):
            h = g * hg + r
            seg = acum[:, h:h + 1] - acum_t[h:h + 1, :]
            decay = jnp.exp(jnp.where(causal, seg, -jnp.inf))
            scores.append((cb * decay * dt_t[h:h + 1, :]).astype(BF16))
            if r % 2 == 1:
                j = r // 2
                lo_ = g * gw + j * LANES
                rhs = _pair_rhs(xs[:, lo_:lo_ + LANES], mask_l, mask_r)
                lhs = jnp.concatenate([scores[2 * j], scores[2 * j + 1]], axis=1)
                yield
                y_parts.append(jnp.dot(lhs, rhs, preferred_element_type=F32)
                               + y_off[:, j * LANES:(j + 1) * LANES] * ea_x[:, lo_:lo_ + LANES])
    state_ref[...] = _kept(state_ref[...] * ea_x[CHUNK - 1:CHUNK, :] + jnp.concatenate(s_parts, axis=1), keep)

    outs = []
    for g in range(SSD_GROUPS):
        cols = slice(g * gw, (g + 1) * gw)
        yg = jnp.concatenate(y_parts[g * (hg // 2):(g + 1) * (hg // 2)], axis=1) + xs[:, cols] * dskip_ref[:, cols]
        yg = yg * _silu(z[:, cols])
        ms = jnp.mean(yg * yg, axis=-1, keepdims=True)
        outs.append(yg * lax.rsqrt(ms + EPS) * nw_ref[:, cols])
        yield
    emit(jnp.concatenate(outs, axis=1).astype(BF16))


def _s5_chunk(emit, keep, u, gate, bmat_ref, cmat_ref, dinv_re_ref, dinv_im_ref, dfwd_re_ref, dfwd_im_ref,
              tri_ref, dskip_ref, wglu_ref, bglu_ref, carry_re_ref, carry_im_ref, x_ref):
    u_b = u.astype(BF16)
    nsub = CHUNK // S5_SUB
    halves = []
    for a in range(2):
        u_a = u_b[:, a * S5_HALF_CH:(a + 1) * S5_HALF_CH]
        halves.append(jnp.concatenate([jnp.dot(u_a, bmat_ref[a, j], preferred_element_type=F32)
                                       for j in range(2 * S5_HALF_STATE // UNIT_COLS)], axis=1))
        yield
    zs = []
    for a in range(2):
        lanes = slice(a * S5_HALF_STATE, (a + 1) * S5_HALF_STATE)
        bu_re = halves[a][:, :S5_HALF_STATE]
        bu_im = halves[a][:, S5_HALF_STATE:]
        di_re = jnp.concatenate([dinv_re_ref[:, lanes]] * nsub, axis=0)
        di_im = jnp.concatenate([dinv_im_ref[:, lanes]] * nsub, axis=0)
        bs = jnp.concatenate([bu_re * di_re - bu_im * di_im, bu_re * di_im + bu_im * di_re], axis=1).astype(BF16)
        yield
        zs.append(jnp.dot(tri_ref[...], bs, preferred_element_type=F32))
    for a in range(2):
        lanes = slice(a * S5_HALF_STATE, (a + 1) * S5_HALF_STATE)
        cr = carry_re_ref[:, lanes]
        ci = carry_im_ref[:, lanes]
        df_re = dfwd_re_ref[:, lanes]
        df_im = dfwd_im_ref[:, lanes]
        for m in range(nsub):
            rows = slice(m * S5_SUB, (m + 1) * S5_SUB)
            zr = zs[a][rows, :S5_HALF_STATE] + cr
            zi = zs[a][rows, S5_HALF_STATE:] + ci
            xr = zr * df_re - zi * df_im
            xi = zr * df_im + zi * df_re
            cr = xr[S5_SUB - 1:S5_SUB, :]
            ci = xi[S5_SUB - 1:S5_SUB, :]
            x_ref[rows, lanes] = xr.astype(BF16)
            x_ref[rows, S5_PLANE + a * S5_HALF_STATE:S5_PLANE + (a + 1) * S5_HALF_STATE] = xi.astype(BF16)
            if m % 4 == 3:
                yield
        carry_re_ref[:, lanes] = _kept(cr, keep)
        carry_im_ref[:, lanes] = _kept(ci, keep)
    ys = []
    for a in range(2):
        xa = jnp.concatenate([x_ref[:, a * S5_HALF_STATE:(a + 1) * S5_HALF_STATE],
                              x_ref[:, S5_PLANE + a * S5_HALF_STATE:S5_PLANE + (a + 1) * S5_HALF_STATE]], axis=1)
        ys.append(jnp.dot(xa, cmat_ref[a], preferred_element_type=F32))
        yield
    y = jnp.concatenate(ys, axis=1) + dskip_ref[...] * u
    y = 0.5 * y * (1.0 + jnp.tanh(math.sqrt(2.0 / math.pi) * (y + 0.044715 * (y * y * y))))
    y_b = y.astype(BF16)
    yield
    glu = jnp.dot(y_b, wglu_ref[...], preferred_element_type=F32) + bglu_ref[...]
    emit((y * _sigmoid(glu) * _silu(gate)).astype(BF16))


def _ret_chunk(emit, keep, q_raw, k_raw, v, gate, cos_ref, sin_ref, qdec_ref, kdect_ref, dmat_ref, headmask_ref,
               kvmask_ref, cdec_ref, mask_l, mask_r, seg_ref, nw_ref, state_ref):
    half = RET_KEY_DIM // 2
    lane = lax.broadcasted_iota(jnp.int32, (CHUNK, RET_QK), 1)
    first_half = (lane % RET_KEY_DIM) < half
    cos = cos_ref[...]
    sin = sin_ref[...]

    def rotary(t):
        swapped = jnp.where(first_half, pltpu.roll(t, RET_QK - half, 1), pltpu.roll(t, half, 1))
        return t * cos + swapped * sin

    q = rotary(q_raw)
    yield
    k = rotary(k_raw) * (RET_KEY_DIM ** -0.5)
    v_b = v.astype(BF16)
    k_t = k.T
    kt_b = k_t.astype(BF16)
    yield
    q_b = q.astype(BF16)
    k_heads = jnp.concatenate([kt_b] * RET_HEADS, axis=1) * headmask_ref[...]
    qd_b = (q * qdec_ref[...]).astype(BF16)
    state_b = state_ref[...].astype(BF16)
    kd_b = (k_t * kdect_ref[...]).astype(BF16)
    yield
    scores = jnp.dot(q_b, k_heads, preferred_element_type=F32)
    y_cross = jnp.dot(qd_b, state_b, preferred_element_type=F32)
    kv = jnp.dot(kd_b, v_b, preferred_element_type=F32)
    yield
    state_ref[...] = _kept(state_ref[...] * cdec_ref[...] + kv * kvmask_ref[...], keep)
    y_parts = []
    for j in range(RET_HEADS // 2):
        sa = (scores[:, (2 * j) * CHUNK:(2 * j + 1) * CHUNK] * dmat_ref[2 * j]).astype(BF16)
        sb = (scores[:, (2 * j + 1) * CHUNK:(2 * j + 2) * CHUNK] * dmat_ref[2 * j + 1]).astype(BF16)
        rhs = _pair_rhs(v[:, j * LANES:(j + 1) * LANES], mask_l, mask_r)
        lhs = jnp.concatenate([sa, sb], axis=1)
        yield
        y_parts.append(jnp.dot(lhs, rhs, preferred_element_type=F32))
    y = jnp.concatenate(y_parts, axis=1) + y_cross
    yy_b = (y * y).astype(BF16)
    yield
    ms = jnp.dot(yy_b, seg_ref[...], preferred_element_type=F32)
    emit((y * lax.rsqrt(ms + EPS) * nw_ref[...] * _silu(gate)).astype(BF16))


N_SSD, N_S5, N_RET = 9, 10, 8


class _Filler:
    def __init__(self):
        self.units = []

    def add(self, fn):
        self.units.append(fn)

    def __call__(self, n=1):
        for _ in range(min(n, len(self.units))):
            self.units.pop(0)()


UNIT_COLS = 256


def _layer_kernel(final, nc, xin_ref, xres_ref, nw_ref, win_ref, cos_a_ref, sin_a_ref, cos_b_ref, sin_b_ref, *rest):
    ssd_p = rest[:N_SSD]
    s5_p = rest[N_SSD:N_SSD + N_S5]
    ret_p = rest[N_SSD + N_S5:N_SSD + N_S5 + N_RET]
    wout_ref, fnw_ref, out_ref = rest[N_SSD + N_S5 + N_RET:N_SSD + N_S5 + N_RET + 3]
    scratch = rest[N_SSD + N_S5 + N_RET + 3:]
    proj = dict(zip([n for n, _ in PIECES], scratch[:len(PIECES)]))
    y_ref, h_ref, xprev_ref, ssd_state_ref, carry_re_ref, carry_im_ref, s5x_ref, ret_state_ref = scratch[len(PIECES):]
    s = pl.program_id(0)

    @pl.when(s == 0)
    def _():
        for ref in list(proj.values()) + [y_ref, xprev_ref, ssd_state_ref, carry_re_ref, carry_im_ref,
                                          ret_state_ref]:
            ref[...] = jnp.zeros_like(ref)

    convw, convb, dtb, alog, dskip, ssd_nw, tri3, expand, pairmask = ssd_p
    qdec, kdect, dmat, headmask, kvmask, cdec, seg, ret_nw = ret_p
    mask_l = pairmask[0:1, :]
    mask_r = pairmask[1:2, :]

    def half_step(rows, wr, rd, cos_ref, sin_ref, keep):
        fill = _Filler()

        def out_unit(c0):
            def run():
                out_ref[rows, c0:c0 + UNIT_COLS] = xres_ref[rows, c0:c0 + UNIT_COLS] + jnp.dot(
                    y_ref[wr], wout_ref[c0 // UNIT_COLS], preferred_element_type=F32)
            return run

        def in_unit(name, c0, c1):
            off, _ = PIECE_OFF[name]

            def run():
                proj[name][wr, :, c0:c1] = jnp.dot(h_ref[wr], win_ref[:, off + c0:off + c1],
                                                   preferred_element_type=F32)
            return run

        for c0 in range(0, D_MODEL, UNIT_COLS):
            fill.add(out_unit(c0))
        fill(1)
        x = xin_ref[rows, :]
        r = lax.rsqrt(jnp.mean(x * x, axis=-1, keepdims=True) + EPS)
        h_ref[wr] = (x * r * nw_ref[...]).astype(BF16)
        for name, width in PIECES:
            for c0 in range(0, width, UNIT_COLS):
                fill.add(in_unit(name, c0, min(c0 + UNIT_COLS, width)))

        def emit_to(c0, c1):
            def emit(value):
                y_ref[rd, :, c0:c1] = value
            return emit

        branches = [
            _ssd_chunk(emit_to(0, SSD_WIDTH), keep, proj["z"][rd], proj["xbc"][rd], proj["dt"][rd], convw, convb,
                       dtb, alog, dskip, ssd_nw, tri3, expand, mask_l, mask_r, xprev_ref, ssd_state_ref),
            _s5_chunk(emit_to(SSD_WIDTH, SSD_WIDTH + S5_WIDTH), keep, proj["s5u"][rd], proj["s5g"][rd], *s5_p,
                      carry_re_ref, carry_im_ref, s5x_ref),
            _ret_chunk(emit_to(SSD_WIDTH + S5_WIDTH, MIX_WIDTH), keep, proj["q"][rd], proj["k"][rd], proj["v"][rd],
                       proj["rg"][rd], cos_ref, sin_ref, qdec, kdect, dmat, headmask, kvmask, cdec, mask_l, mask_r,
                       seg, ret_nw, ret_state_ref),
        ]
        while branches:
            for gen in list(branches):
                if next(gen, "done") == "done":
                    branches.remove(gen)
            fill(1)
        fill(len(fill.units))

    keep_a = jnp.where((2 * s) % nc == 0, 0.0, 1.0).astype(F32)
    half_step(slice(0, CHUNK), 0, 1, cos_a_ref, sin_a_ref, keep_a)
    half_step(slice(CHUNK, 2 * CHUNK), 1, 0, cos_b_ref, sin_b_ref, None)
    if final:
        acc = out_ref[...]
        rr = lax.rsqrt(jnp.mean(acc * acc, axis=-1, keepdims=True) + EPS)
        out_ref[...] = acc * rr * fnw_ref[...]


def _ssd_constants():
    tri = np.tril(np.ones((CHUNK, CHUNK), np.float32))
    tri3 = np.concatenate([tri, tri, tri], axis=1)
    expand = np.zeros((2 * LANES, SSD_WIDTH), np.float32)
    for h in range(SSD_HEADS):
        expand[h, h * SSD_HEAD_DIM:(h + 1) * SSD_HEAD_DIM] = 1.0
        expand[LANES + h, h * SSD_HEAD_DIM:(h + 1) * SSD_HEAD_DIM] = 1.0
    pair = np.zeros((2, LANES), np.float32)
    pair[0, :SSD_HEAD_DIM] = 1.0
    pair[1, SSD_HEAD_DIM:] = 1.0
    return jnp.asarray(tri3, BF16), jnp.asarray(expand, BF16), jnp.asarray(pair, F32)


def _pad_lanes(v):
    return jnp.pad(v.reshape(1, -1), ((0, 0), (0, DT_PAD - v.shape[-1])))


def _s5_tables(lam_re, lam_im, b_re, b_im, c_re, c_im, log_step):
    step = jnp.exp(log_step.astype(F32))[:, None]
    lam_re = lam_re.astype(F32)
    lam_im = lam_im.astype(F32)
    mag = jnp.exp(lam_re * step)
    ang = lam_im * step
    lb_re = mag * jnp.cos(ang)
    lb_im = mag * jnp.sin(ang)
    den = lam_re * lam_re + lam_im * lam_im
    f_re = ((lb_re - 1.0) * lam_re + lb_im * lam_im) / den
    f_im = (lb_im * lam_re - (lb_re - 1.0) * lam_im) / den
    bb_re = f_re[..., None] * b_re - f_im[..., None] * b_im
    bb_im = f_re[..., None] * b_im + f_im[..., None] * b_re
    eye = jnp.eye(S5_HALF_GROUPS, dtype=F32)

    def in_block(bb):
        bb = bb.reshape(2, S5_HALF_GROUPS, S5_STATE, S5_GROUP_CH)
        return jnp.einsum("agpc,gh->agchp", bb, eye).reshape(2, S5_HALF_CH, S5_HALF_STATE)

    def out_block(cc):
        cc = cc.reshape(2, S5_HALF_GROUPS, S5_GROUP_CH, S5_STATE)
        return jnp.einsum("agcp,gh->agphc", cc, eye).reshape(2, S5_HALF_STATE, S5_HALF_CH)

    bmat = jnp.concatenate([in_block(bb_re), in_block(bb_im)], axis=2).astype(BF16)
    bmat = bmat.reshape(2, S5_HALF_CH, 2 * S5_HALF_STATE // UNIT_COLS, UNIT_COLS).transpose(0, 2, 1, 3)
    cmat = jnp.concatenate([out_block(c_re), out_block(-c_im)], axis=1).astype(BF16)
    k = jnp.arange(1, S5_SUB + 1, dtype=F32)[:, None, None]
    la = (lam_re * step)[None]
    an = ang[None]
    dfwd_re = (jnp.exp(k * la) * jnp.cos(k * an)).reshape(S5_SUB, S5_PLANE)
    dfwd_im = (jnp.exp(k * la) * jnp.sin(k * an)).reshape(S5_SUB, S5_PLANE)
    dinv_re = (jnp.exp(-k * la) * jnp.cos(k * an)).reshape(S5_SUB, S5_PLANE)
    dinv_im = (-jnp.exp(-k * la) * jnp.sin(k * an)).reshape(S5_SUB, S5_PLANE)
    return bmat, cmat, dinv_re, dinv_im, dfwd_re, dfwd_im


def _ret_constants(l):
    pos = jnp.arange(l, dtype=F32)
    inv_freq = ROPE_BASE ** (-jnp.arange(0, RET_KEY_DIM, 2, dtype=F32) / RET_KEY_DIM)
    ang = pos[:, None] * inv_freq[None, :]
    cos, sin = jnp.cos(ang), jnp.sin(ang)
    cos_t = jnp.tile(jnp.concatenate([cos, cos], axis=1), (1, RET_HEADS))
    sin_t = jnp.tile(jnp.concatenate([-sin, sin], axis=1), (1, RET_HEADS))
    log_g = jnp.log1p(-jnp.exp2(-5.0 - jnp.arange(RET_HEADS, dtype=F32)))
    idx = jnp.arange(CHUNK, dtype=F32)
    diff = idx[:, None] - idx[None, :]
    dmat = jnp.where(diff >= 0, jnp.exp(jnp.maximum(diff, 0.0) * log_g[:, None, None]), 0.0)
    qdec = jnp.repeat(jnp.exp((idx + 1.0)[:, None] * log_g[None, :]), RET_KEY_DIM, axis=1)
    kdect = jnp.repeat(jnp.exp((CHUNK - 1.0 - idx)[:, None] * log_g[None, :]), RET_KEY_DIM, axis=1).T
    cdec = jnp.repeat(jnp.exp(CHUNK * log_g), RET_VAL_DIM).reshape(1, RET_WIDTH)
    headmask = np.kron(np.eye(RET_HEADS, dtype=np.float32), np.ones((RET_KEY_DIM, CHUNK), np.float32))
    kvmask = np.kron(np.eye(RET_HEADS, dtype=np.float32), np.ones((RET_KEY_DIM, RET_VAL_DIM), np.float32))
    seg = np.kron(np.eye(RET_HEADS, dtype=np.float32),
                  np.full((RET_VAL_DIM, RET_VAL_DIM), 1.0 / RET_VAL_DIM, np.float32))
    return (cos_t, sin_t, qdec, kdect, dmat, jnp.asarray(headmask, BF16), jnp.asarray(kvmask, F32), cdec,
            jnp.asarray(seg, BF16))


def _pack_w_in(w):
    o = SSD_WIDTH + SSD_CONV_CH
    n_dt = SSD_HEADS
    w = w.astype(BF16)
    return jnp.concatenate([w[:, :o + n_dt], jnp.zeros((w.shape[0], DT_PAD - n_dt), BF16), w[:, o + n_dt:]], axis=1)


def _const_spec(shape):
    nd = len(shape)
    return pl.BlockSpec(shape, lambda *_: (0,) * nd, pipeline_mode=pl.Buffered(1))


def _layer(x, final, norm_w, w_in, ssd_params, s5_params, ret_params, cos_t, sin_t, w_out, final_norm_w):
    b, l, d = x.shape
    consts = [norm_w.reshape(1, d), _pack_w_in(w_in)]
    w_out_tiles = w_out.astype(BF16).reshape(MIX_WIDTH, d // UNIT_COLS, UNIT_COLS).transpose(1, 0, 2)
    tail = list(ssd_params) + list(s5_params) + list(ret_params) + [w_out_tiles, final_norm_w.reshape(1, d)]
    nc = l // CHUNK
    assert nc % 2 == 0, "sequences must hold an even number of chunks"
    nb = b * nc // 2

    def block_spec(lag):
        def index(s):
            t = jnp.clip(s - lag, 0, nb - 1)
            return (t // (nc // 2), t % (nc // 2), 0)
        return pl.BlockSpec((None, 2 * CHUNK, d), index)

    def pos_spec(lag):
        return pl.BlockSpec((CHUNK, RET_QK), lambda s: (jnp.clip(2 * s - lag, 0, 2 * nb - 1) % nc, 0))

    return pl.pallas_call(
        functools.partial(_layer_kernel, final, nc),
        grid=(nb + 1,),
        in_specs=[block_spec(0), block_spec(1)] + [_const_spec(p.shape) for p in consts]
        + [pos_spec(1), pos_spec(1), pos_spec(0), pos_spec(0)] + [_const_spec(p.shape) for p in tail],
        out_specs=block_spec(1),
        out_shape=jax.ShapeDtypeStruct((b, l, d), F32),
        scratch_shapes=[pltpu.VMEM((2, CHUNK, w), F32) for _, w in PIECES]
        + [pltpu.VMEM((2, CHUNK, MIX_WIDTH), BF16),
           pltpu.VMEM((2, CHUNK, D_MODEL), BF16),
           pltpu.VMEM((SUBLANES, SSD_CONV_CH), F32),
           pltpu.VMEM((SSD_STATE, SSD_WIDTH), F32),
           pltpu.VMEM((1, S5_PLANE), F32), pltpu.VMEM((1, S5_PLANE), F32),
           pltpu.VMEM((CHUNK, 2 * S5_PLANE), BF16),
           pltpu.VMEM((RET_QK, RET_WIDTH), F32)],
        compiler_params=pltpu.CompilerParams(dimension_semantics=("arbitrary",), vmem_limit_bytes=VMEM_LIMIT),
        name="layer",
    )(x, x, *consts, cos_t, sin_t, cos_t, sin_t, *tail)


def kernel(x, norm_w, w_in, conv_w, conv_b, dt_bias, a_log, d_ssd, ssd_norm_w, s5_lambda_re, s5_lambda_im,
           s5_b_re, s5_b_im, s5_c_re, s5_c_im, s5_d, s5_log_step, s5_w_glu, s5_b_glu, ret_norm_w, w_out,
           final_norm_w):
    b, l, d = x.shape
    depth = w_in.shape[0]
    tri3, expand, pair = _ssd_constants()
    cos_t, sin_t, qdec, kdect, dmat, headmask, kvmask, cdec, seg = _ret_constants(l)
    s5tri = jnp.asarray(np.kron(np.eye(CHUNK // S5_SUB, dtype=np.float32),
                                np.tril(np.ones((S5_SUB, S5_SUB), np.float32))), BF16)
    h_res = x.astype(F32)
    for i in range(depth):
        ssd_params = [conv_w[i], conv_b[i].reshape(1, -1), _pad_lanes(dt_bias[i]), _pad_lanes(a_log[i]),
                      jnp.repeat(d_ssd[i], SSD_HEAD_DIM).reshape(1, SSD_WIDTH), ssd_norm_w[i].reshape(1, -1),
                      tri3, expand, pair]
        bmat, cmat, dinv_re, dinv_im, dfwd_re, dfwd_im = _s5_tables(
            s5_lambda_re[i], s5_lambda_im[i], s5_b_re[i], s5_b_im[i], s5_c_re[i], s5_c_im[i], s5_log_step[i])
        s5_params = [bmat, cmat, dinv_re, dinv_im, dfwd_re, dfwd_im, s5tri, s5_d[i].reshape(1, -1),
                     s5_w_glu[i].astype(BF16), s5_b_glu[i].reshape(1, -1)]
        ret_params = [qdec, kdect, dmat, headmask, kvmask, cdec, seg, ret_norm_w[i].reshape(1, -1)]
        h_res = _layer(h_res, i == depth - 1, norm_w[i], w_in[i], ssd_params, s5_params, ret_params, cos_t, sin_t,
                       w_out[i], final_norm_w)
    return h_res.astype(x.dtype)
```

```python
import functools
import math

import numpy as np
import jax
import jax.numpy as jnp
from jax import lax
from jax.experimental import pallas as pl
from jax.experimental.pallas import tpu as pltpu

F32 = jnp.float32
BF16 = jnp.bfloat16

D_MODEL = 1024
CHUNK = 128
EPS = 1e-6
SSD_HEADS = 16
SSD_HEAD_DIM = 64
SSD_WIDTH = SSD_HEADS * SSD_HEAD_DIM
SSD_GROUPS = 2
SSD_STATE = 128
CONV_WIDTH = 4
SSD_BC = SSD_GROUPS * SSD_STATE
SSD_CONV_CH = SSD_WIDTH + 2 * SSD_BC
S5_GROUP_CH = 16
S5_GROUPS = 32
S5_WIDTH = S5_GROUPS * S5_GROUP_CH
S5_STATE = 64
S5_HALF_GROUPS = S5_GROUPS // 2
S5_HALF_CH = S5_WIDTH // 2
S5_HALF_STATE = S5_HALF_GROUPS * S5_STATE
S5_PLANE = S5_GROUPS * S5_STATE
S5_SUB = 16
RET_HEADS = 8
RET_KEY_DIM = 32
RET_VAL_DIM = 64
RET_QK = RET_HEADS * RET_KEY_DIM
RET_WIDTH = RET_HEADS * RET_VAL_DIM
ROPE_BASE = 10000.0
MIX_WIDTH = SSD_WIDTH + S5_WIDTH + RET_WIDTH
LANES = 128
SUBLANES = 8
DT_PAD = LANES
PIECES = (("z", SSD_WIDTH), ("xbc", SSD_CONV_CH), ("dt", DT_PAD), ("s5g", S5_WIDTH), ("s5u", S5_WIDTH),
          ("q", RET_QK), ("k", RET_QK), ("v", RET_WIDTH), ("rg", RET_WIDTH))
PIECE_OFF = {}
_o = 0
for _n, _w in PIECES:
    PIECE_OFF[_n] = (_o, _w)
    _o += _w
IN_PACKED = _o
VMEM_LIMIT = 56 * 1024 * 1024


def _silu(x):
    h = 0.5 * x
    return h + h * jnp.tanh(h)


def _sigmoid(x):
    return 0.5 + 0.5 * jnp.tanh(0.5 * x)


def _split3(x):
    p1 = x.astype(BF16)
    r1 = x - p1.astype(F32)
    p2 = r1.astype(BF16)
    p3 = (r1 - p2.astype(F32)).astype(BF16)
    return p1, p2, p3


def _pair_rhs(xp, mask_l, mask_r):
    return jnp.concatenate([(xp * mask_l).astype(BF16), (xp * mask_r).astype(BF16)], axis=0)


def _kept(value, keep):
    return value if keep is None else value * keep


def _ssd_chunk(emit, keep, z, xbc_raw, dt_raw, convw_ref, convb_ref, dtb_ref, alog_ref, dskip_ref, nw_ref,
               tri3_ref, expand_ref, mask_l, mask_r, xprev_ref, state_ref):
    xprev = xprev_ref[...]
    row8 = lax.broadcasted_iota(jnp.int32, (SUBLANES, UNIT_COLS), 0)
    blocks = []
    for c0 in range(0, SSD_CONV_CH, UNIT_COLS):
        cols = slice(c0, c0 + UNIT_COLS)
        xb = xbc_raw[:, cols]
        acc = convb_ref[:, cols] + convw_ref[CONV_WIDTH - 1:CONV_WIDTH, cols] * xb
        for j in range(1, CONV_WIDTH):
            rolled = pltpu.roll(xb, j, 0)
            head = jnp.where(row8 < j, pltpu.roll(xprev[:, cols], j, 0), rolled[0:SUBLANES, :])
            shifted = jnp.concatenate([head, rolled[SUBLANES:, :]], axis=0)
            acc = acc + convw_ref[CONV_WIDTH - 1 - j:CONV_WIDTH - j, cols] * shifted
        blocks.append(_silu(acc))
        yield
    xprev_ref[...] = _kept(xbc_raw[CHUNK - SUBLANES:, :], keep)
    xbc = jnp.concatenate(blocks, axis=1)
    xs = xbc[:, :SSD_WIDTH]
    bm = xbc[:, SSD_WIDTH:SSD_WIDTH + SSD_BC]
    cm = xbc[:, SSD_WIDTH + SSD_BC:]

    dtr = dt_raw + dtb_ref[...]
    dt = jnp.maximum(dtr, 0.0) + jnp.log1p(jnp.exp(-jnp.abs(dtr)))
    a = -jnp.exp(alog_ref[...])
    dta3 = jnp.concatenate(_split3(dt * a), axis=0)
    yield
    acum = jnp.dot(tri3_ref[...], dta3, preferred_element_type=F32)
    acum_t = acum.T
    dt_t = dt.T
    ea = jnp.exp(acum)
    wgt = jnp.exp(acum[CHUNK - 1:CHUNK, :] - acum) * dt

    def hi_lo(v):
        hi = v.astype(BF16)
        lo = (v - hi.astype(F32)).astype(BF16)
        return jnp.concatenate([hi, lo], axis=1)

    wgt_hl = hi_lo(wgt)
    ea_hl = hi_lo(ea)
    yield
    wgt_x = jnp.dot(wgt_hl, expand_ref[...], preferred_element_type=F32)
    ea_x = jnp.dot(ea_hl, expand_ref[...], preferred_element_type=F32)

    xw_b = (xs * wgt_x).astype(BF16)
    row = lax.broadcasted_iota(jnp.int32, (CHUNK, CHUNK), 0)
    col = lax.broadcasted_iota(jnp.int32, (CHUNK, CHUNK), 1)
    causal = row >= col
    hg = SSD_HEADS // SSD_GROUPS
    gw = hg * SSD_HEAD_DIM
    y_parts = []
    s_parts = []
    for g in range(SSD_GROUPS):
        bm_g = bm[:, g * SSD_STATE:(g + 1) * SSD_STATE]
        cm_b = cm[:, g * SSD_STATE:(g + 1) * SSD_STATE].astype(BF16)
        bm_b = bm_g.astype(BF16)
        bm_tb = bm_g.T.astype(BF16)
        state_b = state_ref[:, g * gw:(g + 1) * gw].astype(BF16)
        yield
        cb = lax.dot_general(cm_b, bm_b, (((1,), (1,)), ((), ())), preferred_element_type=F32)
        s_parts.append(jnp.dot(bm_tb, xw_b[:, g * gw:(g + 1) * gw], preferred_element_type=F32))
        y_off = jnp.dot(cm_b, state_b, preferred_element_type=F32)
        scores = []
        for r in range(hg):
            h = g * hg + r
            seg = acum[:, h:h + 1] - acum_t[h:h + 1, :]
            decay = jnp.exp(jnp.where(causal, seg, -jnp.inf))
            scores.append((cb * decay * dt_t[h:h + 1, :]).astype(BF16))
            if r % 2 == 1:
                j = r // 2
                lo_ = g * gw + j * LANES
                rhs = _pair_rhs(xs[:, lo_:lo_ + LANES], mask_l, mask_r)
                lhs = jnp.concatenate([scores[2 * j], scores[2 * j + 1]], axis=1)
                yield
                y_parts.append(jnp.dot(lhs, rhs, preferred_element_type=F32)
                               + y_off[:, j * LANES:(j + 1) * LANES] * ea_x[:, lo_:lo_ + LANES])
    state_ref[...] = _kept(state_ref[...] * ea_x[CHUNK - 1:CHUNK, :] + jnp.concatenate(s_parts, axis=1), keep)

    outs = []
    for g in range(SSD_GROUPS):
        cols = slice(g * gw, (g + 1) * gw)
        yg = jnp.concatenate(y_parts[g * (hg // 2):(g + 1) * (hg // 2)], axis=1) + xs[:, cols] * dskip_ref[:, cols]
        yg = yg * _silu(z[:, cols])
        ms = jnp.mean(yg * yg, axis=-1, keepdims=True)
        outs.append(yg * lax.rsqrt(ms + EPS) * nw_ref[:, cols])
        yield
    emit(jnp.concatenate(outs, axis=1).astype(BF16))


def _s5_chunk(emit, keep, u, gate, bmat_ref, cmat_ref, dinv_re_ref, dinv_im_ref, dfwd_re_ref, dfwd_im_ref,
              tri_ref, dskip_ref, wglu_ref, bglu_ref, carry_re_ref, carry_im_ref, x_ref):
    u_b = u.astype(BF16)
    nsub = CHUNK // S5_SUB
    halves = []
    for a in range(2):
        u_a = u_b[:, a * S5_HALF_CH:(a + 1) * S5_HALF_CH]
        halves.append(jnp.concatenate([jnp.dot(u_a, bmat_ref[a, j], preferred_element_type=F32)
                                       for j in range(2 * S5_HALF_STATE // UNIT_COLS)], axis=1))
        yield
    zs = []
    for a in range(2):
        lanes = slice(a * S5_HALF_STATE, (a + 1) * S5_HALF_STATE)
        bu_re = halves[a][:, :S5_HALF_STATE]
        bu_im = halves[a][:, S5_HALF_STATE:]
        di_re = jnp.concatenate([dinv_re_ref[:, lanes]] * nsub, axis=0)
        di_im = jnp.concatenate([dinv_im_ref[:, lanes]] * nsub, axis=0)
        bs = jnp.concatenate([bu_re * di_re - bu_im * di_im, bu_re * di_im + bu_im * di_re], axis=1).astype(BF16)
        yield
        zs.append(jnp.dot(tri_ref[...], bs, preferred_element_type=F32))
    for a in range(2):
        lanes = slice(a * S5_HALF_STATE, (a + 1) * S5_HALF_STATE)
        cr = carry_re_ref[:, lanes]
        ci = carry_im_ref[:, lanes]
        df_re = dfwd_re_ref[:, lanes]
        df_im = dfwd_im_ref[:, lanes]
        for m in range(nsub):
            rows = slice(m * S5_SUB, (m + 1) * S5_SUB)
            zr = zs[a][rows, :S5_HALF_STATE] + cr
            zi = zs[a][rows, S5_HALF_STATE:] + ci
            xr = zr * df_re - zi * df_im
            xi = zr * df_im + zi * df_re
            cr = xr[S5_SUB - 1:S5_SUB, :]
            ci = xi[S5_SUB - 1:S5_SUB, :]
            x_ref[rows, lanes] = xr.astype(BF16)
            x_ref[rows, S5_PLANE + a * S5_HALF_STATE:S5_PLANE + (a + 1) * S5_HALF_STATE] = xi.astype(BF16)
            if m % 4 == 3:
                yield
        carry_re_ref[:, lanes] = _kept(cr, keep)
        carry_im_ref[:, lanes] = _kept(ci, keep)
    ys = []
    for a in range(2):
        xa = jnp.concatenate([x_ref[:, a * S5_HALF_STATE:(a + 1) * S5_HALF_STATE],
                              x_ref[:, S5_PLANE + a * S5_HALF_STATE:S5_PLANE + (a + 1) * S5_HALF_STATE]], axis=1)
        ys.append(jnp.dot(xa, cmat_ref[a], preferred_element_type=F32))
        yield
    y = jnp.concatenate(ys, axis=1) + dskip_ref[...] * u
    y = 0.5 * y * (1.0 + jnp.tanh(math.sqrt(2.0 / math.pi) * (y + 0.044715 * (y * y * y))))
    y_b = y.astype(BF16)
    yield
    glu = jnp.dot(y_b, wglu_ref[...], preferred_element_type=F32) + bglu_ref[...]
    emit((y * _sigmoid(glu) * _silu(gate)).astype(BF16))


def _ret_chunk(emit, keep, q_raw, k_raw, v, gate, cos_ref, sin_ref, qdec_ref, kdect_ref, dmat_ref, headmask_ref,
               kvmask_ref, cdec_ref, mask_l, mask_r, seg_ref, nw_ref, state_ref):
    half = RET_KEY_DIM // 2
    lane = lax.broadcasted_iota(jnp.int32, (CHUNK, RET_QK), 1)
    first_half = (lane % RET_KEY_DIM) < half
    cos = cos_ref[...]
    sin = sin_ref[...]

    def rotary(t):
        swapped = jnp.where(first_half, pltpu.roll(t, RET_QK - half, 1), pltpu.roll(t, half, 1))
        return t * cos + swapped * sin

    q = rotary(q_raw)
    yield
    k = rotary(k_raw) * (RET_KEY_DIM ** -0.5)
    v_b = v.astype(BF16)
    k_t = k.T
    kt_b = k_t.astype(BF16)
    yield
    q_b = q.astype(BF16)
    k_heads = jnp.concatenate([kt_b] * RET_HEADS, axis=1) * headmask_ref[...]
    qd_b = (q * qdec_ref[...]).astype(BF16)
    state_b = state_ref[...].astype(BF16)
    kd_b = (k_t * kdect_ref[...]).astype(BF16)
    yield
    scores = jnp.dot(q_b, k_heads, preferred_element_type=F32)
    y_cross = jnp.dot(qd_b, state_b, preferred_element_type=F32)
    kv = jnp.dot(kd_b, v_b, preferred_element_type=F32)
    yield
    state_ref[...] = _kept(state_ref[...] * cdec_ref[...] + kv * kvmask_ref[...], keep)
    y_parts = []
    for j in range(RET_HEADS // 2):
        sa = (scores[:, (2 * j) * CHUNK:(2 * j + 1) * CHUNK] * dmat_ref[2 * j]).astype(BF16)
        sb = (scores[:, (2 * j + 1) * CHUNK:(2 * j + 2) * CHUNK] * dmat_ref[2 * j + 1]).astype(BF16)
        rhs = _pair_rhs(v[:, j * LANES:(j + 1) * LANES], mask_l, mask_r)
        lhs = jnp.concatenate([sa, sb], axis=1)
        yield
        y_parts.append(jnp.dot(lhs, rhs, preferred_element_type=F32))
    y = jnp.concatenate(y_parts, axis=1) + y_cross
    yy_b = (y * y).astype(BF16)
    yield
    ms = jnp.dot(yy_b, seg_ref[...], preferred_element_type=F32)
    emit((y * lax.rsqrt(ms + EPS) * nw_ref[...] * _silu(gate)).astype(BF16))


N_SSD, N_S5, N_RET = 9, 10, 8


class _Filler:
    def __init__(self):
        self.units = []

    def add(self, fn):
        self.units.append(fn)

    def __call__(self, n=1):
        for _ in range(min(n, len(self.units))):
            self.units.pop(0)()


UNIT_COLS = 256


def _layer_kernel(final, nc, xin_ref, xres_ref, nw_ref, win_ref, cos_a_ref, sin_a_ref, cos_b_ref, sin_b_ref, *rest):
    ssd_p = rest[:N_SSD]
    s5_p = rest[N_SSD:N_SSD + N_S5]
    ret_p = rest[N_SSD + N_S5:N_SSD + N_S5 + N_RET]
    wout_ref, fnw_ref, out_ref = rest[N_SSD + N_S5 + N_RET:N_SSD + N_S5 + N_RET + 3]
    scratch = rest[N_SSD + N_S5 + N_RET + 3:]
    proj = dict(zip([n for n, _ in PIECES], scratch[:len(PIECES)]))
    y_ref, h_ref, xprev_ref, ssd_state_ref, carry_re_ref, carry_im_ref, s5x_ref, ret_state_ref = scratch[len(PIECES):]
    s = pl.program_id(0)

    @pl.when(s == 0)
    def _():
        for ref in list(proj.values()) + [y_ref, xprev_ref, ssd_state_ref, carry_re_ref, carry_im_ref,
                                          ret_state_ref]:
            ref[...] = jnp.zeros_like(ref)

    convw, convb, dtb, alog, dskip, ssd_nw, tri3, expand, pairmask = ssd_p
    qdec, kdect, dmat, headmask, kvmask, cdec, seg, ret_nw = ret_p
    mask_l = pairmask[0:1, :]
    mask_r = pairmask[1:2, :]

    def half_step(rows, wr, rd, cos_ref, sin_ref, keep):
        fill = _Filler()

        def out_unit(c0):
            def run():
                out_ref[rows, c0:c0 + UNIT_COLS] = xres_ref[rows, c0:c0 + UNIT_COLS] + jnp.dot(
                    y_ref[wr], wout_ref[c0 // UNIT_COLS], preferred_element_type=F32)
            return run

        def in_unit(name, c0, c1):
            off, _ = PIECE_OFF[name]

            def run():
                proj[name][wr, :, c0:c1] = jnp.dot(h_ref[wr], win_ref[:, off + c0:off + c1],
                                                   preferred_element_type=F32)
            return run

        for c0 in range(0, D_MODEL, UNIT_COLS):
            fill.add(out_unit(c0))
        fill(1)
        x = xin_ref[rows, :]
        r = lax.rsqrt(jnp.mean(x * x, axis=-1, keepdims=True) + EPS)
        h_ref[wr] = (x * r * nw_ref[...]).astype(BF16)
        for name, width in PIECES:
            for c0 in range(0, width, UNIT_COLS):
                fill.add(in_unit(name, c0, min(c0 + UNIT_COLS, width)))

        def emit_to(c0, c1):
            def emit(value):
                y_ref[rd, :, c0:c1] = value
            return emit

        branches = [
            _ssd_chunk(emit_to(0, SSD_WIDTH), keep, proj["z"][rd], proj["xbc"][rd], proj["dt"][rd], convw, convb,
                       dtb, alog, dskip, ssd_nw, tri3, expand, mask_l, mask_r, xprev_ref, ssd_state_ref),
            _s5_chunk(emit_to(SSD_WIDTH, SSD_WIDTH + S5_WIDTH), keep, proj["s5u"][rd], proj["s5g"][rd], *s5_p,
                      carry_re_ref, carry_im_ref, s5x_ref),
            _ret_chunk(emit_to(SSD_WIDTH + S5_WIDTH, MIX_WIDTH), keep, proj["q"][rd], proj["k"][rd], proj["v"][rd],
                       proj["rg"][rd], cos_ref, sin_ref, qdec, kdect, dmat, headmask, kvmask, cdec, mask_l, mask_r,
                       seg, ret_nw, ret_state_ref),
        ]
        while branches:
            for gen in list(branches):
                if next(gen, "done") == "done":
                    branches.remove(gen)
            fill(1)
        fill(len(fill.units))

    keep_a = jnp.where((2 * s) % nc == 0, 0.0, 1.0).astype(F32)
    half_step(slice(0, CHUNK), 0, 1, cos_a_ref, sin_a_ref, keep_a)
    half_step(slice(CHUNK, 2 * CHUNK), 1, 0, cos_b_ref, sin_b_ref, None)
    if final:
        acc = out_ref[...]
        rr = lax.rsqrt(jnp.mean(acc * acc, axis=-1, keepdims=True) + EPS)
        out_ref[...] = acc * rr * fnw_ref[...]


def _ssd_constants():
    tri = np.tril(np.ones((CHUNK, CHUNK), np.float32))
    tri3 = np.concatenate([tri, tri, tri], axis=1)
    expand = np.zeros((2 * LANES, SSD_WIDTH), np.float32)
    for h in range(SSD_HEADS):
        expand[h, h * SSD_HEAD_DIM:(h + 1) * SSD_HEAD_DIM] = 1.0
        expand[LANES + h, h * SSD_HEAD_DIM:(h + 1) * SSD_HEAD_DIM] = 1.0
    pair = np.zeros((2, LANES), np.float32)
    pair[0, :SSD_HEAD_DIM] = 1.0
    pair[1, SSD_HEAD_DIM:] = 1.0
    return jnp.asarray(tri3, BF16), jnp.asarray(expand, BF16), jnp.asarray(pair, F32)


def _s5_tables(lam_re, lam_im, b_re, b_im, c_re, c_im, log_step):
    step = jnp.exp(log_step.astype(F32))[:, None]
    lam_re = lam_re.astype(F32)
    lam_im = lam_im.astype(F32)
    mag = jnp.exp(lam_re * step)
    ang = lam_im * step
    lb_re = mag * jnp.cos(ang)
    lb_im = mag * jnp.sin(ang)
    den = lam_re * lam_re + lam_im * lam_im
    f_re = ((lb_re - 1.0) * lam_re + lb_im * lam_im) / den
    f_im = (lb_im * lam_re - (lb_re - 1.0) * lam_im) / den
    bb_re = f_re[..., None] * b_re - f_im[..., None] * b_im
    bb_im = f_re[..., None] * b_im + f_im[..., None] * b_re
    eye = jnp.eye(S5_HALF_GROUPS, dtype=F32)

    def in_block(bb):
        bb = bb.reshape(2, S5_HALF_GROUPS, S5_STATE, S5_GROUP_CH)
        return jnp.einsum("agpc,gh->agchp", bb, eye).reshape(2, S5_HALF_CH, S5_HALF_STATE)

    def out_block(cc):
        cc = cc.reshape(2, S5_HALF_GROUPS, S5_GROUP_CH, S5_STATE)
        return jnp.einsum("agcp,gh->agphc", cc, eye).reshape(2, S5_HALF_STATE, S5_HALF_CH)

    bmat = jnp.concatenate([in_block(bb_re), in_block(bb_im)], axis=2).astype(BF16)
    bmat = bmat.reshape(2, S5_HALF_CH, 2 * S5_HALF_STATE // UNIT_COLS, UNIT_COLS).transpose(0, 2, 1, 3)
    cmat = jnp.concatenate([out_block(c_re), out_block(-c_im)], axis=1).astype(BF16)
    k = jnp.arange(1, S5_SUB + 1, dtype=F32)[:, None, None]
    la = (lam_re * step)[None]
    an = ang[None]
    dfwd_re = (jnp.exp(k * la) * jnp.cos(k * an)).reshape(S5_SUB, S5_PLANE)
    dfwd_im = (jnp.exp(k * la) * jnp.sin(k * an)).reshape(S5_SUB, S5_PLANE)
    dinv_re = (jnp.exp(-k * la) * jnp.cos(k * an)).reshape(S5_SUB, S5_PLANE)
    dinv_im = (-jnp.exp(-k * la) * jnp.sin(k * an)).reshape(S5_SUB, S5_PLANE)
    return bmat, cmat, dinv_re, dinv_im, dfwd_re, dfwd_im


def _ret_constants(l):
    pos = jnp.arange(l, dtype=F32)
    inv_freq = ROPE_BASE ** (-jnp.arange(0, RET_KEY_DIM, 2, dtype=F32) / RET_KEY_DIM)
    ang = pos[:, None] * inv_freq[None, :]
    cos, sin = jnp.cos(ang), jnp.sin(ang)
    cos_t = jnp.tile(jnp.concatenate([cos, cos], axis=1), (1, RET_HEADS))
    sin_t = jnp.tile(jnp.concatenate([-sin, sin], axis=1), (1, RET_HEADS))
    log_g = np.log1p(-np.exp2(-5.0 - np.arange(RET_HEADS, dtype=np.float64)))
    idx = np.arange(CHUNK, dtype=np.float64)
    diff = idx[:, None] - idx[None, :]
    dmat = np.where(diff >= 0, np.exp(np.maximum(diff, 0.0) * log_g[:, None, None]), 0.0)
    qdec = np.repeat(np.exp((idx + 1.0)[:, None] * log_g[None, :]), RET_KEY_DIM, axis=1)
    kdect = np.repeat(np.exp((CHUNK - 1.0 - idx)[:, None] * log_g[None, :]), RET_KEY_DIM, axis=1).T
    cdec = np.repeat(np.exp(CHUNK * log_g), RET_VAL_DIM).reshape(1, RET_WIDTH)
    headmask = np.kron(np.eye(RET_HEADS), np.ones((RET_KEY_DIM, CHUNK)))
    kvmask = np.kron(np.eye(RET_HEADS), np.ones((RET_KEY_DIM, RET_VAL_DIM)))
    seg = np.kron(np.eye(RET_HEADS), np.full((RET_VAL_DIM, RET_VAL_DIM), 1.0 / RET_VAL_DIM))
    return (cos_t, sin_t, jnp.asarray(qdec, F32), jnp.asarray(kdect, F32), jnp.asarray(dmat, F32),
            jnp.asarray(headmask, BF16), jnp.asarray(kvmask, F32), jnp.asarray(cdec, F32), jnp.asarray(seg, BF16))


def _pack_w_in(w):
    o = SSD_WIDTH + SSD_CONV_CH
    n_dt = SSD_HEADS
    w = w.astype(BF16)
    pad = jnp.zeros(w.shape[:2] + (DT_PAD - n_dt,), BF16)
    return jnp.concatenate([w[..., :o + n_dt], pad, w[..., o + n_dt:]], axis=-1)


def _const_spec(shape):
    nd = len(shape)
    return pl.BlockSpec(shape, lambda *_: (0,) * nd, pipeline_mode=pl.Buffered(1))


class _PerLayer:
    def __init__(self, array):
        self.array = array

    def spec(self, layer):
        shape = self.array.shape[1:]
        return pl.BlockSpec((None,) + shape, lambda *_: (layer,) + (0,) * len(shape), pipeline_mode=pl.Buffered(1))


def _spec(p, layer):
    return p.spec(layer) if isinstance(p, _PerLayer) else _const_spec(p.shape)


def _array(p):
    return p.array if isinstance(p, _PerLayer) else p


def _layer(x, layer, final, consts, tail, cos_t, sin_t):
    b, l, d = x.shape
    nc = l // CHUNK
    assert nc % 2 == 0, "sequences must hold an even number of chunks"
    nb = b * nc // 2

    def block_spec(lag):
        def index(s):
            t = jnp.clip(s - lag, 0, nb - 1)
            return (t // (nc // 2), t % (nc // 2), 0)
        return pl.BlockSpec((None, 2 * CHUNK, d), index)

    def pos_spec(lag):
        return pl.BlockSpec((CHUNK, RET_QK), lambda s: (jnp.clip(2 * s - lag, 0, 2 * nb - 1) % nc, 0))

    return pl.pallas_call(
        functools.partial(_layer_kernel, final, nc),
        grid=(nb + 1,),
        in_specs=[block_spec(0), block_spec(1)] + [_spec(p, layer) for p in consts]
        + [pos_spec(1), pos_spec(1), pos_spec(0), pos_spec(0)] + [_spec(p, layer) for p in tail],
        out_specs=block_spec(1),
        out_shape=jax.ShapeDtypeStruct((b, l, d), F32),
        scratch_shapes=[pltpu.VMEM((2, CHUNK, w), F32) for _, w in PIECES]
        + [pltpu.VMEM((2, CHUNK, MIX_WIDTH), BF16),
           pltpu.VMEM((2, CHUNK, D_MODEL), BF16),
           pltpu.VMEM((SUBLANES, SSD_CONV_CH), F32),
           pltpu.VMEM((SSD_STATE, SSD_WIDTH), F32),
           pltpu.VMEM((1, S5_PLANE), F32), pltpu.VMEM((1, S5_PLANE), F32),
           pltpu.VMEM((CHUNK, 2 * S5_PLANE), BF16),
           pltpu.VMEM((RET_QK, RET_WIDTH), F32)],
        compiler_params=pltpu.CompilerParams(dimension_semantics=("arbitrary",), vmem_limit_bytes=VMEM_LIMIT),
        name="layer",
    )(x, x, *map(_array, consts), cos_t, sin_t, cos_t, sin_t, *map(_array, tail))


def kernel(x, norm_w, w_in, conv_w, conv_b, dt_bias, a_log, d_ssd, ssd_norm_w, s5_lambda_re, s5_lambda_im,
           s5_b_re, s5_b_im, s5_c_re, s5_c_im, s5_d, s5_log_step, s5_w_glu, s5_b_glu, ret_norm_w, w_out,
           final_norm_w):
    b, l, d = x.shape
    depth = w_in.shape[0]
    tri3, expand, pair = _ssd_constants()
    cos_t, sin_t, qdec, kdect, dmat, headmask, kvmask, cdec, seg = _ret_constants(l)
    s5tri = jnp.asarray(np.kron(np.eye(CHUNK // S5_SUB, dtype=np.float32),
                                np.tril(np.ones((S5_SUB, S5_SUB), np.float32))), BF16)
    row = lambda p: _PerLayer(p.reshape(depth, 1, -1))
    pad_row = lambda p: _PerLayer(jnp.pad(p, ((0, 0), (0, DT_PAD - p.shape[-1]))).reshape(depth, 1, DT_PAD))
    consts = [row(norm_w), _PerLayer(_pack_w_in(w_in))]
    ssd_params = [_PerLayer(conv_w), row(conv_b), pad_row(dt_bias), pad_row(a_log),
                  row(jnp.repeat(d_ssd, SSD_HEAD_DIM, axis=-1)), row(ssd_norm_w), tri3, expand, pair]
    s5_tables = jax.vmap(_s5_tables)(s5_lambda_re, s5_lambda_im, s5_b_re, s5_b_im, s5_c_re, s5_c_im, s5_log_step)
    s5_params = [_PerLayer(t) for t in s5_tables] + [s5tri, row(s5_d), _PerLayer(s5_w_glu.astype(BF16)),
                                                      row(s5_b_glu)]
    ret_params = [qdec, kdect, dmat, headmask, kvmask, cdec, seg, row(ret_norm_w)]
    w_out_tiles = w_out.astype(BF16).reshape(depth, MIX_WIDTH, d // UNIT_COLS, UNIT_COLS).transpose(0, 2, 1, 3)
    tail = ssd_params + s5_params + ret_params + [_PerLayer(w_out_tiles), final_norm_w.reshape(1, d)]
    h_res = x.astype(F32)
    for i in range(depth):
        h_res = _layer(h_res, i, i == depth - 1, consts, tail, cos_t, sin_t)
    return h_res.astype(x.dtype)
```

```python
import functools
import math

import numpy as np
import jax
import jax.numpy as jnp
from jax import lax
from jax.experimental import pallas as pl
from jax.experimental.pallas import tpu as pltpu

F32 = jnp.float32
BF16 = jnp.bfloat16

D_MODEL = 1024
CHUNK = 128
EPS = 1e-6
SSD_HEADS = 16
SSD_HEAD_DIM = 64
SSD_WIDTH = SSD_HEADS * SSD_HEAD_DIM
SSD_GROUPS = 2
SSD_STATE = 128
CONV_WIDTH = 4
SSD_BC = SSD_GROUPS * SSD_STATE
SSD_CONV_CH = SSD_WIDTH + 2 * SSD_BC
S5_GROUP_CH = 16
S5_GROUPS = 32
S5_WIDTH = S5_GROUPS * S5_GROUP_CH
S5_STATE = 64
S5_HALF_GROUPS = S5_GROUPS // 2
S5_HALF_CH = S5_WIDTH // 2
S5_HALF_STATE = S5_HALF_GROUPS * S5_STATE
S5_PLANE = S5_GROUPS * S5_STATE
S5_SUB = 16
RET_HEADS = 8
RET_KEY_DIM = 32
RET_VAL_DIM = 64
RET_QK = RET_HEADS * RET_KEY_DIM
RET_WIDTH = RET_HEADS * RET_VAL_DIM
ROPE_BASE = 10000.0
MIX_WIDTH = SSD_WIDTH + S5_WIDTH + RET_WIDTH
LANES = 128
SUBLANES = 8
DT_PAD = LANES
PIECES = (("z", SSD_WIDTH), ("xbc", SSD_CONV_CH), ("dt", DT_PAD), ("s5g", S5_WIDTH), ("s5u", S5_WIDTH),
          ("q", RET_QK), ("k", RET_QK), ("v", RET_WIDTH), ("rg", RET_WIDTH))
PIECE_OFF = {}
_o = 0
for _n, _w in PIECES:
    PIECE_OFF[_n] = (_o, _w)
    _o += _w
IN_PACKED = _o
VMEM_LIMIT = 56 * 1024 * 1024


def _silu(x):
    h = 0.5 * x
    return h + h * jnp.tanh(h)


def _sigmoid(x):
    return 0.5 + 0.5 * jnp.tanh(0.5 * x)


def _split3(x):
    p1 = x.astype(BF16)
    r1 = x - p1.astype(F32)
    p2 = r1.astype(BF16)
    p3 = (r1 - p2.astype(F32)).astype(BF16)
    return p1, p2, p3


def _pair_rhs(xp, mask_l, mask_r):
    return jnp.concatenate([(xp * mask_l).astype(BF16), (xp * mask_r).astype(BF16)], axis=0)


def _kept(value, keep):
    return value if keep is None else value * keep


def _ssd_chunk(emit, keep, z, xbc_raw, dt_raw, convw_ref, convb_ref, dtb_ref, alog_ref, dskip_ref, nw_ref,
               tri3_ref, mask_l, mask_r, xprev_ref, state_ref):
    xprev = xprev_ref[...]
    row8 = lax.broadcasted_iota(jnp.int32, (SUBLANES, UNIT_COLS), 0)
    blocks = []
    for c0 in range(0, SSD_CONV_CH, UNIT_COLS):
        cols = slice(c0, c0 + UNIT_COLS)
        xb = xbc_raw[:, cols]
        acc = convb_ref[:, cols] + convw_ref[CONV_WIDTH - 1:CONV_WIDTH, cols] * xb
        for j in range(1, CONV_WIDTH):
            rolled = pltpu.roll(xb, j, 0)
            head = jnp.where(row8 < j, pltpu.roll(xprev[:, cols], j, 0), rolled[0:SUBLANES, :])
            shifted = jnp.concatenate([head, rolled[SUBLANES:, :]], axis=0)
            acc = acc + convw_ref[CONV_WIDTH - 1 - j:CONV_WIDTH - j, cols] * shifted
        blocks.append(_silu(acc))
        yield
    xprev_ref[...] = _kept(xbc_raw[CHUNK - SUBLANES:, :], keep)
    xbc = jnp.concatenate(blocks, axis=1)
    xs = xbc[:, :SSD_WIDTH]
    bm = xbc[:, SSD_WIDTH:SSD_WIDTH + SSD_BC]
    cm = xbc[:, SSD_WIDTH + SSD_BC:]

    dtr = dt_raw + dtb_ref[...]
    dt = jnp.maximum(dtr, 0.0) + jnp.log1p(jnp.exp(-jnp.abs(dtr)))
    a = -jnp.exp(alog_ref[...])
    dta3 = jnp.concatenate(_split3(dt * a), axis=0)
    yield
    acum = jnp.dot(tri3_ref[...], dta3, preferred_element_type=F32)
    acum_t = acum.T
    dt_t = dt.T
    ea = jnp.exp(acum)
    wgt = jnp.exp(acum[CHUNK - 1:CHUNK, :] - acum) * dt

    left_head = lax.broadcasted_iota(jnp.int32, (CHUNK, LANES), 1) < SSD_HEAD_DIM

    def expand(v):
        tiles = []
        for j in range(SSD_HEADS // 2):
            tiles.append(jnp.where(left_head, jnp.broadcast_to(v[:, 2 * j:2 * j + 1], (CHUNK, LANES)),
                                   jnp.broadcast_to(v[:, 2 * j + 1:2 * j + 2], (CHUNK, LANES))))
        return jnp.concatenate(tiles, axis=1)

    wgt_x = expand(wgt)
    ea_x = expand(ea)
    yield

    xw_b = (xs * wgt_x).astype(BF16)
    row = lax.broadcasted_iota(jnp.int32, (CHUNK, CHUNK), 0)
    col = lax.broadcasted_iota(jnp.int32, (CHUNK, CHUNK), 1)
    causal = row >= col
    hg = SSD_HEADS // SSD_GROUPS
    gw = hg * SSD_HEAD_DIM
    y_parts = []
    s_parts = []
    for g in range(SSD_GROUPS):
        bm_g = bm[:, g * SSD_STATE:(g + 1) * SSD_STATE]
        cm_b = cm[:, g * SSD_STATE:(g + 1) * SSD_STATE].astype(BF16)
        bm_b = bm_g.astype(BF16)
        bm_tb = bm_g.T.astype(BF16)
        state_b = state_ref[:, g * gw:(g + 1) * gw].astype(BF16)
        yield
        cb = lax.dot_general(cm_b, bm_b, (((1,), (1,)), ((), ())), preferred_element_type=F32)
        s_parts.append(jnp.dot(bm_tb, xw_b[:, g * gw:(g + 1) * gw], preferred_element_type=F32))
        y_off = jnp.dot(cm_b, state_b, preferred_element_type=F32)
        scores = []
        for r in range(hg):
            h = g * hg + r
            seg = acum[:, h:h + 1] - acum_t[h:h + 1, :]
            decay = jnp.exp(jnp.where(causal, seg, -jnp.inf))
            scores.append((cb * decay * dt_t[h:h + 1, :]).astype(BF16))
            if r % 2 == 1:
                j = r // 2
                lo_ = g * gw + j * LANES
                rhs = _pair_rhs(xs[:, lo_:lo_ + LANES], mask_l, mask_r)
                lhs = jnp.concatenate([scores[2 * j], scores[2 * j + 1]], axis=1)
                yield
                y_parts.append(jnp.dot(lhs, rhs, preferred_element_type=F32)
                               + y_off[:, j * LANES:(j + 1) * LANES] * ea_x[:, lo_:lo_ + LANES])
    state_ref[...] = _kept(state_ref[...] * ea_x[CHUNK - 1:CHUNK, :] + jnp.concatenate(s_parts, axis=1), keep)

    outs = []
    for g in range(SSD_GROUPS):
        cols = slice(g * gw, (g + 1) * gw)
        yg = jnp.concatenate(y_parts[g * (hg // 2):(g + 1) * (hg // 2)], axis=1) + xs[:, cols] * dskip_ref[:, cols]
        yg = yg * _silu(z[:, cols])
        ms = jnp.mean(yg * yg, axis=-1, keepdims=True)
        outs.append(yg * lax.rsqrt(ms + EPS) * nw_ref[:, cols])
        yield
    emit(jnp.concatenate(outs, axis=1).astype(BF16))


def _s5_chunk(emit, keep, u, gate, bmat_ref, cmat_ref, dinv_re_ref, dinv_im_ref, dfwd_re_ref, dfwd_im_ref,
              tri_ref, dskip_ref, wglu_ref, bglu_ref, carry_re_ref, carry_im_ref, x_ref):
    u_b = u.astype(BF16)
    nsub = CHUNK // S5_SUB
    halves = []
    for a in range(2):
        u_a = u_b[:, a * S5_HALF_CH:(a + 1) * S5_HALF_CH]
        halves.append(jnp.concatenate([jnp.dot(u_a, bmat_ref[a, j], preferred_element_type=F32)
                                       for j in range(2 * S5_HALF_STATE // UNIT_COLS)], axis=1))
        yield
    zs = []
    for a in range(2):
        lanes = slice(a * S5_HALF_STATE, (a + 1) * S5_HALF_STATE)
        bu_re = halves[a][:, :S5_HALF_STATE]
        bu_im = halves[a][:, S5_HALF_STATE:]
        di_re = jnp.concatenate([dinv_re_ref[:, lanes]] * nsub, axis=0)
        di_im = jnp.concatenate([dinv_im_ref[:, lanes]] * nsub, axis=0)
        bs = jnp.concatenate([bu_re * di_re - bu_im * di_im, bu_re * di_im + bu_im * di_re], axis=1).astype(BF16)
        yield
        zs.append(jnp.dot(tri_ref[...], bs, preferred_element_type=F32))
    for a in range(2):
        lanes = slice(a * S5_HALF_STATE, (a + 1) * S5_HALF_STATE)
        cr = carry_re_ref[:, lanes]
        ci = carry_im_ref[:, lanes]
        df_re = dfwd_re_ref[:, lanes]
        df_im = dfwd_im_ref[:, lanes]
        for m in range(nsub):
            rows = slice(m * S5_SUB, (m + 1) * S5_SUB)
            zr = zs[a][rows, :S5_HALF_STATE] + cr
            zi = zs[a][rows, S5_HALF_STATE:] + ci
            xr = zr * df_re - zi * df_im
            xi = zr * df_im + zi * df_re
            cr = xr[S5_SUB - 1:S5_SUB, :]
            ci = xi[S5_SUB - 1:S5_SUB, :]
            x_ref[rows, lanes] = xr.astype(BF16)
            x_ref[rows, S5_PLANE + a * S5_HALF_STATE:S5_PLANE + (a + 1) * S5_HALF_STATE] = xi.astype(BF16)
            if m % 4 == 3:
                yield
        carry_re_ref[:, lanes] = _kept(cr, keep)
        carry_im_ref[:, lanes] = _kept(ci, keep)
    ys = []
    for a in range(2):
        xa = jnp.concatenate([x_ref[:, a * S5_HALF_STATE:(a + 1) * S5_HALF_STATE],
                              x_ref[:, S5_PLANE + a * S5_HALF_STATE:S5_PLANE + (a + 1) * S5_HALF_STATE]], axis=1)
        ys.append(jnp.dot(xa, cmat_ref[a], preferred_element_type=F32))
        yield
    y = jnp.concatenate(ys, axis=1) + dskip_ref[...] * u
    y = 0.5 * y * (1.0 + jnp.tanh(math.sqrt(2.0 / math.pi) * (y + 0.044715 * (y * y * y))))
    y_b = y.astype(BF16)
    yield
    glu = jnp.dot(y_b, wglu_ref[...], preferred_element_type=F32) + bglu_ref[...]
    emit((y * _sigmoid(glu) * _silu(gate)).astype(BF16))


def _ret_chunk(emit, keep, q_raw, k_raw, v, gate, cos_ref, sin_ref, qdec_ref, kdect_ref, dmat_ref, headmask_ref,
               kvmask_ref, cdec_ref, mask_l, mask_r, seg_ref, nw_ref, state_ref):
    half = RET_KEY_DIM // 2
    lane = lax.broadcasted_iota(jnp.int32, (CHUNK, RET_QK), 1)
    first_half = (lane % RET_KEY_DIM) < half
    cos = cos_ref[...]
    sin = sin_ref[...]

    def rotary(t):
        swapped = jnp.where(first_half, pltpu.roll(t, RET_QK - half, 1), pltpu.roll(t, half, 1))
        return t * cos + swapped * sin

    q = rotary(q_raw)
    yield
    k = rotary(k_raw) * (RET_KEY_DIM ** -0.5)
    v_b = v.astype(BF16)
    k_t = k.T
    kt_b = k_t.astype(BF16)
    yield
    q_b = q.astype(BF16)
    k_heads = jnp.concatenate([kt_b] * RET_HEADS, axis=1) * headmask_ref[...]
    qd_b = (q * qdec_ref[...]).astype(BF16)
    state_b = state_ref[...].astype(BF16)
    kd_b = (k_t * kdect_ref[...]).astype(BF16)
    yield
    scores = jnp.dot(q_b, k_heads, preferred_element_type=F32)
    y_cross = jnp.dot(qd_b, state_b, preferred_element_type=F32)
    kv = jnp.dot(kd_b, v_b, preferred_element_type=F32)
    yield
    state_ref[...] = _kept(state_ref[...] * cdec_ref[...] + kv * kvmask_ref[...], keep)
    y_parts = []
    for j in range(RET_HEADS // 2):
        sa = (scores[:, (2 * j) * CHUNK:(2 * j + 1) * CHUNK] * dmat_ref[2 * j]).astype(BF16)
        sb = (scores[:, (2 * j + 1) * CHUNK:(2 * j + 2) * CHUNK] * dmat_ref[2 * j + 1]).astype(BF16)
        rhs = _pair_rhs(v[:, j * LANES:(j + 1) * LANES], mask_l, mask_r)
        lhs = jnp.concatenate([sa, sb], axis=1)
        yield
        y_parts.append(jnp.dot(lhs, rhs, preferred_element_type=F32))
    y = jnp.concatenate(y_parts, axis=1) + y_cross
    yy_b = (y * y).astype(BF16)
    yield
    ms = jnp.dot(yy_b, seg_ref[...], preferred_element_type=F32)
    emit((y * lax.rsqrt(ms + EPS) * nw_ref[...] * _silu(gate)).astype(BF16))


N_SSD, N_S5, N_RET = 8, 10, 8


class _Filler:
    def __init__(self):
        self.units = []

    def add(self, fn):
        self.units.append(fn)

    def __call__(self, n=1):
        for _ in range(min(n, len(self.units))):
            self.units.pop(0)()


UNIT_COLS = 256


def _layer_kernel(final, nc, xin_ref, xres_ref, nw_ref, win_ref, cos_a_ref, sin_a_ref, cos_b_ref, sin_b_ref, *rest):
    ssd_p = rest[:N_SSD]
    s5_p = rest[N_SSD:N_SSD + N_S5]
    ret_p = rest[N_SSD + N_S5:N_SSD + N_S5 + N_RET]
    wout_ref, fnw_ref, out_ref = rest[N_SSD + N_S5 + N_RET:N_SSD + N_S5 + N_RET + 3]
    scratch = rest[N_SSD + N_S5 + N_RET + 3:]
    proj = dict(zip([n for n, _ in PIECES], scratch[:len(PIECES)]))
    y_ref, h_ref, xprev_ref, ssd_state_ref, carry_re_ref, carry_im_ref, s5x_ref, ret_state_ref = scratch[len(PIECES):]
    s = pl.program_id(0)

    @pl.when(s == 0)
    def _():
        for ref in list(proj.values()) + [y_ref, xprev_ref, ssd_state_ref, carry_re_ref, carry_im_ref,
                                          ret_state_ref]:
            ref[...] = jnp.zeros_like(ref)

    convw, convb, dtb, alog, dskip, ssd_nw, tri3, pairmask = ssd_p
    qdec, kdect, dmat, headmask, kvmask, cdec, seg, ret_nw = ret_p
    mask_l = pairmask[0:1, :]
    mask_r = pairmask[1:2, :]

    def half_step(rows, wr, rd, cos_ref, sin_ref, keep):
        fill = _Filler()

        def out_unit(c0):
            def run():
                out_ref[rows, c0:c0 + UNIT_COLS] = xres_ref[rows, c0:c0 + UNIT_COLS] + jnp.dot(
                    y_ref[wr], wout_ref[c0 // UNIT_COLS], preferred_element_type=F32)
            return run

        def in_unit(name, c0, c1):
            off, _ = PIECE_OFF[name]

            def run():
                proj[name][wr, :, c0:c1] = jnp.dot(h_ref[wr], win_ref[:, off + c0:off + c1],
                                                   preferred_element_type=F32)
            return run

        for c0 in range(0, D_MODEL, UNIT_COLS):
            fill.add(out_unit(c0))
        fill(1)
        x = xin_ref[rows, :]
        r = lax.rsqrt(jnp.mean(x * x, axis=-1, keepdims=True) + EPS)
        h_ref[wr] = (x * r * nw_ref[...]).astype(BF16)
        for name, width in PIECES:
            for c0 in range(0, width, UNIT_COLS):
                fill.add(in_unit(name, c0, min(c0 + UNIT_COLS, width)))

        def emit_to(c0, c1):
            def emit(value):
                y_ref[rd, :, c0:c1] = value
            return emit

        branches = [
            _ssd_chunk(emit_to(0, SSD_WIDTH), keep, proj["z"][rd], proj["xbc"][rd], proj["dt"][rd], convw, convb,
                       dtb, alog, dskip, ssd_nw, tri3, mask_l, mask_r, xprev_ref, ssd_state_ref),
            _s5_chunk(emit_to(SSD_WIDTH, SSD_WIDTH + S5_WIDTH), keep, proj["s5u"][rd], proj["s5g"][rd], *s5_p,
                      carry_re_ref, carry_im_ref, s5x_ref),
            _ret_chunk(emit_to(SSD_WIDTH + S5_WIDTH, MIX_WIDTH), keep, proj["q"][rd], proj["k"][rd], proj["v"][rd],
                       proj["rg"][rd], cos_ref, sin_ref, qdec, kdect, dmat, headmask, kvmask, cdec, mask_l, mask_r,
                       seg, ret_nw, ret_state_ref),
        ]
        while branches:
            for gen in list(branches):
                if next(gen, "done") == "done":
                    branches.remove(gen)
            fill(1)
        fill(len(fill.units))

    keep_a = jnp.where((2 * s) % nc == 0, 0.0, 1.0).astype(F32)
    half_step(slice(0, CHUNK), 0, 1, cos_a_ref, sin_a_ref, keep_a)
    half_step(slice(CHUNK, 2 * CHUNK), 1, 0, cos_b_ref, sin_b_ref, None)
    if final:
        acc = out_ref[...]
        rr = lax.rsqrt(jnp.mean(acc * acc, axis=-1, keepdims=True) + EPS)
        out_ref[...] = acc * rr * fnw_ref[...]


def _ssd_constants():
    tri = np.tril(np.ones((CHUNK, CHUNK), np.float32))
    tri3 = np.concatenate([tri, tri, tri], axis=1)
    pair = np.zeros((2, LANES), np.float32)
    pair[0, :SSD_HEAD_DIM] = 1.0
    pair[1, SSD_HEAD_DIM:] = 1.0
    return jnp.asarray(tri3, BF16), jnp.asarray(pair, F32)


def _s5_tables(lam_re, lam_im, b_re, b_im, c_re, c_im, log_step):
    step = jnp.exp(log_step.astype(F32))[:, None]
    lam_re = lam_re.astype(F32)
    lam_im = lam_im.astype(F32)
    mag = jnp.exp(lam_re * step)
    ang = lam_im * step
    lb_re = mag * jnp.cos(ang)
    lb_im = mag * jnp.sin(ang)
    den = lam_re * lam_re + lam_im * lam_im
    f_re = ((lb_re - 1.0) * lam_re + lb_im * lam_im) / den
    f_im = (lb_im * lam_re - (lb_re - 1.0) * lam_im) / den
    bb_re = f_re[..., None] * b_re - f_im[..., None] * b_im
    bb_im = f_re[..., None] * b_im + f_im[..., None] * b_re
    eye = jnp.eye(S5_HALF_GROUPS, dtype=F32)

    def in_block(bb):
        bb = bb.reshape(2, S5_HALF_GROUPS, S5_STATE, S5_GROUP_CH)
        return jnp.einsum("agpc,gh->agchp", bb, eye).reshape(2, S5_HALF_CH, S5_HALF_STATE)

    def out_block(cc):
        cc = cc.reshape(2, S5_HALF_GROUPS, S5_GROUP_CH, S5_STATE)
        return jnp.einsum("agcp,gh->agphc", cc, eye).reshape(2, S5_HALF_STATE, S5_HALF_CH)

    bmat = jnp.concatenate([in_block(bb_re), in_block(bb_im)], axis=2).astype(BF16)
    bmat = bmat.reshape(2, S5_HALF_CH, 2 * S5_HALF_STATE // UNIT_COLS, UNIT_COLS).transpose(0, 2, 1, 3)
    cmat = jnp.concatenate([out_block(c_re), out_block(-c_im)], axis=1).astype(BF16)
    k = jnp.arange(1, S5_SUB + 1, dtype=F32)[:, None, None]
    la = (lam_re * step)[None]
    an = ang[None]
    dfwd_re = (jnp.exp(k * la) * jnp.cos(k * an)).reshape(S5_SUB, S5_PLANE)
    dfwd_im = (jnp.exp(k * la) * jnp.sin(k * an)).reshape(S5_SUB, S5_PLANE)
    dinv_re = (jnp.exp(-k * la) * jnp.cos(k * an)).reshape(S5_SUB, S5_PLANE)
    dinv_im = (-jnp.exp(-k * la) * jnp.sin(k * an)).reshape(S5_SUB, S5_PLANE)
    return bmat, cmat, dinv_re, dinv_im, dfwd_re, dfwd_im


def _ret_constants(l):
    pos = jnp.arange(l, dtype=F32)
    inv_freq = ROPE_BASE ** (-jnp.arange(0, RET_KEY_DIM, 2, dtype=F32) / RET_KEY_DIM)
    ang = pos[:, None] * inv_freq[None, :]
    cos, sin = jnp.cos(ang), jnp.sin(ang)
    cos_t = jnp.tile(jnp.concatenate([cos, cos], axis=1), (1, RET_HEADS))
    sin_t = jnp.tile(jnp.concatenate([-sin, sin], axis=1), (1, RET_HEADS))
    log_g = np.log1p(-np.exp2(-5.0 - np.arange(RET_HEADS, dtype=np.float64)))
    idx = np.arange(CHUNK, dtype=np.float64)
    diff = idx[:, None] - idx[None, :]
    dmat = np.where(diff >= 0, np.exp(np.maximum(diff, 0.0) * log_g[:, None, None]), 0.0)
    qdec = np.repeat(np.exp((idx + 1.0)[:, None] * log_g[None, :]), RET_KEY_DIM, axis=1)
    kdect = np.repeat(np.exp((CHUNK - 1.0 - idx)[:, None] * log_g[None, :]), RET_KEY_DIM, axis=1).T
    cdec = np.repeat(np.exp(CHUNK * log_g), RET_VAL_DIM).reshape(1, RET_WIDTH)
    headmask = np.kron(np.eye(RET_HEADS), np.ones((RET_KEY_DIM, CHUNK)))
    kvmask = np.kron(np.eye(RET_HEADS), np.ones((RET_KEY_DIM, RET_VAL_DIM)))
    seg = np.kron(np.eye(RET_HEADS), np.full((RET_VAL_DIM, RET_VAL_DIM), 1.0 / RET_VAL_DIM))
    return (cos_t, sin_t, jnp.asarray(qdec, F32), jnp.asarray(kdect, F32), jnp.asarray(dmat, F32),
            jnp.asarray(headmask, BF16), jnp.asarray(kvmask, F32), jnp.asarray(cdec, F32), jnp.asarray(seg, BF16))


def _pack_w_in(w):
    o = SSD_WIDTH + SSD_CONV_CH
    n_dt = SSD_HEADS
    w = w.astype(BF16)
    split = o + n_dt
    head = jnp.pad(w[..., :split], ((0, 0), (0, 0), (0, IN_PACKED - split)))
    tail = jnp.pad(w[..., split:], ((0, 0), (0, 0), (split + DT_PAD - n_dt, 0)))
    return head + tail


def _const_spec(shape):
    nd = len(shape)
    return pl.BlockSpec(shape, lambda *_: (0,) * nd, pipeline_mode=pl.Buffered(1))


class _PerLayer:
    def __init__(self, array):
        self.array = array

    def spec(self, layer):
        shape = self.array.shape[1:]
        return pl.BlockSpec((None,) + shape, lambda *_: (layer,) + (0,) * len(shape), pipeline_mode=pl.Buffered(1))


def _spec(p, layer):
    return p.spec(layer) if isinstance(p, _PerLayer) else _const_spec(p.shape)


def _array(p):
    return p.array if isinstance(p, _PerLayer) else p


def _layer(x, layer, final, consts, tail, cos_t, sin_t):
    b, l, d = x.shape
    nc = l // CHUNK
    assert nc % 2 == 0, "sequences must hold an even number of chunks"
    nb = b * nc // 2

    def block_spec(lag):
        def index(s):
            t = jnp.clip(s - lag, 0, nb - 1)
            return (t // (nc // 2), t % (nc // 2), 0)
        return pl.BlockSpec((None, 2 * CHUNK, d), index)

    def pos_spec(lag):
        return pl.BlockSpec((CHUNK, RET_QK), lambda s: (jnp.clip(2 * s - lag, 0, 2 * nb - 1) % nc, 0))

    return pl.pallas_call(
        functools.partial(_layer_kernel, final, nc),
        grid=(nb + 1,),
        in_specs=[block_spec(0), block_spec(1)] + [_spec(p, layer) for p in consts]
        + [pos_spec(1), pos_spec(1), pos_spec(0), pos_spec(0)] + [_spec(p, layer) for p in tail],
        out_specs=block_spec(1),
        out_shape=jax.ShapeDtypeStruct((b, l, d), F32),
        scratch_shapes=[pltpu.VMEM((2, CHUNK, w), F32) for _, w in PIECES]
        + [pltpu.VMEM((2, CHUNK, MIX_WIDTH), BF16),
           pltpu.VMEM((2, CHUNK, D_MODEL), BF16),
           pltpu.VMEM((SUBLANES, SSD_CONV_CH), F32),
           pltpu.VMEM((SSD_STATE, SSD_WIDTH), F32),
           pltpu.VMEM((1, S5_PLANE), F32), pltpu.VMEM((1, S5_PLANE), F32),
           pltpu.VMEM((CHUNK, 2 * S5_PLANE), BF16),
           pltpu.VMEM((RET_QK, RET_WIDTH), F32)],
        compiler_params=pltpu.CompilerParams(dimension_semantics=("arbitrary",), vmem_limit_bytes=VMEM_LIMIT),
        name="layer",
    )(x, x, *map(_array, consts), cos_t, sin_t, cos_t, sin_t, *map(_array, tail))


def kernel(x, norm_w, w_in, conv_w, conv_b, dt_bias, a_log, d_ssd, ssd_norm_w, s5_lambda_re, s5_lambda_im,
           s5_b_re, s5_b_im, s5_c_re, s5_c_im, s5_d, s5_log_step, s5_w_glu, s5_b_glu, ret_norm_w, w_out,
           final_norm_w):
    b, l, d = x.shape
    depth = w_in.shape[0]
    tri3, pair = _ssd_constants()
    cos_t, sin_t, qdec, kdect, dmat, headmask, kvmask, cdec, seg = _ret_constants(l)
    s5tri = jnp.asarray(np.kron(np.eye(CHUNK // S5_SUB, dtype=np.float32),
                                np.tril(np.ones((S5_SUB, S5_SUB), np.float32))), BF16)
    row = lambda p: _PerLayer(p.reshape(depth, 1, -1))
    pad_row = lambda p: _PerLayer(jnp.pad(p, ((0, 0), (0, DT_PAD - p.shape[-1]))).reshape(depth, 1, DT_PAD))
    consts = [row(norm_w), _PerLayer(_pack_w_in(w_in))]
    ssd_params = [_PerLayer(conv_w), row(conv_b), pad_row(dt_bias), pad_row(a_log),
                  row(jnp.repeat(d_ssd, SSD_HEAD_DIM, axis=-1)), row(ssd_norm_w), tri3, pair]
    s5_tables = jax.vmap(_s5_tables)(s5_lambda_re, s5_lambda_im, s5_b_re, s5_b_im, s5_c_re, s5_c_im, s5_log_step)
    s5_params = [_PerLayer(t) for t in s5_tables] + [s5tri, row(s5_d), _PerLayer(s5_w_glu.astype(BF16)),
                                                      row(s5_b_glu)]
    ret_params = [qdec, kdect, dmat, headmask, kvmask, cdec, seg, row(ret_norm_w)]
    w_out_tiles = w_out.astype(BF16).reshape(depth, MIX_WIDTH, d // UNIT_COLS, UNIT_COLS).transpose(0, 2, 1, 3)
    tail = ssd_params + s5_params + ret_params + [_PerLayer(w_out_tiles), final_norm_w.reshape(1, d)]
    h_res = x.astype(F32)
    for i in range(depth):
        h_res = _layer(h_res, i, i == depth - 1, consts, tail, cos_t, sin_t)
    return h_res.astype(x.dtype)
```

```python
import functools
import math

import numpy as np
import jax
import jax.numpy as jnp
from jax import lax
from jax.experimental import pallas as pl
from jax.experimental.pallas import tpu as pltpu

F32 = jnp.float32
BF16 = jnp.bfloat16

D_MODEL = 1024
CHUNK = 128
EPS = 1e-6
SSD_HEADS = 16
SSD_HEAD_DIM = 64
SSD_WIDTH = SSD_HEADS * SSD_HEAD_DIM
SSD_GROUPS = 2
SSD_STATE = 128
CONV_WIDTH = 4
SSD_BC = SSD_GROUPS * SSD_STATE
SSD_CONV_CH = SSD_WIDTH + 2 * SSD_BC
S5_GROUP_CH = 16
S5_GROUPS = 32
S5_WIDTH = S5_GROUPS * S5_GROUP_CH
S5_STATE = 64
S5_HALF_GROUPS = S5_GROUPS // 2
S5_HALF_CH = S5_WIDTH // 2
S5_HALF_STATE = S5_HALF_GROUPS * S5_STATE
S5_PLANE = S5_GROUPS * S5_STATE
S5_SUB = 16
RET_HEADS = 8
RET_KEY_DIM = 32
RET_VAL_DIM = 64
RET_QK = RET_HEADS * RET_KEY_DIM
RET_WIDTH = RET_HEADS * RET_VAL_DIM
ROPE_BASE = 10000.0
MIX_WIDTH = SSD_WIDTH + S5_WIDTH + RET_WIDTH
LANES = 128
SUBLANES = 8
DT_PAD = LANES
PIECES = (("z", SSD_WIDTH), ("xbc", SSD_CONV_CH), ("dt", DT_PAD), ("s5g", S5_WIDTH), ("s5u", S5_WIDTH),
          ("q", RET_QK), ("k", RET_QK), ("v", RET_WIDTH), ("rg", RET_WIDTH))
PIECE_OFF = {}
_o = 0
for _n, _w in PIECES:
    PIECE_OFF[_n] = (_o, _w)
    _o += _w
IN_PACKED = _o
VMEM_LIMIT = 56 * 1024 * 1024


def _silu(x):
    h = 0.5 * x
    return h + h * jnp.tanh(h)


def _sigmoid(x):
    return 0.5 + 0.5 * jnp.tanh(0.5 * x)


def _split3(x):
    p1 = x.astype(BF16)
    r1 = x - p1.astype(F32)
    p2 = r1.astype(BF16)
    p3 = (r1 - p2.astype(F32)).astype(BF16)
    return p1, p2, p3


def _pair_rhs(xp, mask_l, mask_r):
    return jnp.concatenate([(xp * mask_l).astype(BF16), (xp * mask_r).astype(BF16)], axis=0)


def _kept(value, keep):
    return value if keep is None else value * keep


def _ssd_chunk(emit, keep, z, xbc_raw, dt_raw, convw_ref, convb_ref, dtb_ref, alog_ref, dskip_ref, nw_ref,
               tri3_ref, mask_l, mask_r, xprev_ref, state_ref):
    xprev = xprev_ref[...]
    row8 = lax.broadcasted_iota(jnp.int32, (SUBLANES, UNIT_COLS), 0)
    blocks = []
    for c0 in range(0, SSD_CONV_CH, UNIT_COLS):
        cols = slice(c0, c0 + UNIT_COLS)
        xb = xbc_raw[:, cols]
        acc = convb_ref[:, cols] + convw_ref[CONV_WIDTH - 1:CONV_WIDTH, cols] * xb
        for j in range(1, CONV_WIDTH):
            rolled = pltpu.roll(xb, j, 0)
            head = jnp.where(row8 < j, pltpu.roll(xprev[:, cols], j, 0), rolled[0:SUBLANES, :])
            shifted = jnp.concatenate([head, rolled[SUBLANES:, :]], axis=0)
            acc = acc + convw_ref[CONV_WIDTH - 1 - j:CONV_WIDTH - j, cols] * shifted
        blocks.append(_silu(acc))
        yield
    xprev_ref[...] = _kept(xbc_raw[CHUNK - SUBLANES:, :], keep)
    xbc = jnp.concatenate(blocks, axis=1)
    xs = xbc[:, :SSD_WIDTH]
    bm = xbc[:, SSD_WIDTH:SSD_WIDTH + SSD_BC]
    cm = xbc[:, SSD_WIDTH + SSD_BC:]

    dtr = dt_raw + dtb_ref[...]
    dt = jnp.maximum(dtr, 0.0) + jnp.log1p(jnp.exp(-jnp.abs(dtr)))
    a = -jnp.exp(alog_ref[...])
    dta3 = jnp.concatenate(_split3(dt * a), axis=0)
    yield
    acum = jnp.dot(tri3_ref[...], dta3, preferred_element_type=F32)
    acum_t = acum.T
    dt_t = dt.T
    ea = jnp.exp(acum)
    wgt = jnp.exp(acum[CHUNK - 1:CHUNK, :] - acum) * dt

    left_head = lax.broadcasted_iota(jnp.int32, (CHUNK, LANES), 1) < SSD_HEAD_DIM

    def expand(v):
        tiles = []
        for j in range(SSD_HEADS // 2):
            tiles.append(jnp.where(left_head, jnp.broadcast_to(v[:, 2 * j:2 * j + 1], (CHUNK, LANES)),
                                   jnp.broadcast_to(v[:, 2 * j + 1:2 * j + 2], (CHUNK, LANES))))
        return jnp.concatenate(tiles, axis=1)

    wgt_x = expand(wgt)
    ea_x = expand(ea)
    yield

    xw_b = (xs * wgt_x).astype(BF16)
    row = lax.broadcasted_iota(jnp.int32, (CHUNK, CHUNK), 0)
    col = lax.broadcasted_iota(jnp.int32, (CHUNK, CHUNK), 1)
    causal = row >= col
    hg = SSD_HEADS // SSD_GROUPS
    gw = hg * SSD_HEAD_DIM
    y_parts = []
    s_parts = []
    for g in range(SSD_GROUPS):
        bm_g = bm[:, g * SSD_STATE:(g + 1) * SSD_STATE]
        cm_b = cm[:, g * SSD_STATE:(g + 1) * SSD_STATE].astype(BF16)
        bm_b = bm_g.astype(BF16)
        bm_tb = bm_g.T.astype(BF16)
        state_b = state_ref[:, g * gw:(g + 1) * gw].astype(BF16)
        yield
        cb = lax.dot_general(cm_b, bm_b, (((1,), (1,)), ((), ())), preferred_element_type=F32)
        s_parts.append(jnp.dot(bm_tb, xw_b[:, g * gw:(g + 1) * gw], preferred_element_type=F32))
        y_off = jnp.dot(cm_b, state_b, preferred_element_type=F32)
        scores = []
        for r in range(hg):
            h = g * hg + r
            seg = acum[:, h:h + 1] - acum_t[h:h + 1, :]
            decay = jnp.exp(jnp.where(causal, seg, -jnp.inf))
            scores.append((cb * decay * dt_t[h:h + 1, :]).astype(BF16))
            if r % 2 == 1:
                j = r // 2
                lo_ = g * gw + j * LANES
                rhs = _pair_rhs(xs[:, lo_:lo_ + LANES], mask_l, mask_r)
                lhs = jnp.concatenate([scores[2 * j], scores[2 * j + 1]], axis=1)
                yield
                y_parts.append(jnp.dot(lhs, rhs, preferred_element_type=F32)
                               + y_off[:, j * LANES:(j + 1) * LANES] * ea_x[:, lo_:lo_ + LANES])
    state_ref[...] = _kept(state_ref[...] * ea_x[CHUNK - 1:CHUNK, :] + jnp.concatenate(s_parts, axis=1), keep)

    outs = []
    for g in range(SSD_GROUPS):
        cols = slice(g * gw, (g + 1) * gw)
        yg = jnp.concatenate(y_parts[g * (hg // 2):(g + 1) * (hg // 2)], axis=1) + xs[:, cols] * dskip_ref[:, cols]
        yg = yg * _silu(z[:, cols])
        ms = jnp.mean(yg * yg, axis=-1, keepdims=True)
        outs.append(yg * lax.rsqrt(ms + EPS) * nw_ref[:, cols])
        yield
    emit(jnp.concatenate(outs, axis=1).astype(BF16))


def _s5_chunk(emit, keep, u, gate, bmat_ref, cmat_ref, dinv_re_ref, dinv_im_ref, dfwd_re_ref, dfwd_im_ref,
              tri_ref, dskip_ref, wglu_ref, bglu_ref, carry_re_ref, carry_im_ref, x_ref):
    u_b = u.astype(BF16)
    nsub = CHUNK // S5_SUB
    halves = []
    for a in range(2):
        u_a = u_b[:, a * S5_HALF_CH:(a + 1) * S5_HALF_CH]
        halves.append(jnp.concatenate([jnp.dot(u_a, bmat_ref[a, j], preferred_element_type=F32)
                                       for j in range(2 * S5_HALF_STATE // UNIT_COLS)], axis=1))
        yield
    zs = []
    for a in range(2):
        lanes = slice(a * S5_HALF_STATE, (a + 1) * S5_HALF_STATE)
        bu_re = halves[a][:, :S5_HALF_STATE]
        bu_im = halves[a][:, S5_HALF_STATE:]
        di_re = jnp.concatenate([dinv_re_ref[:, lanes]] * nsub, axis=0)
        di_im = jnp.concatenate([dinv_im_ref[:, lanes]] * nsub, axis=0)
        bs = jnp.concatenate([bu_re * di_re - bu_im * di_im, bu_re * di_im + bu_im * di_re], axis=1).astype(BF16)
        yield
        zs.append(jnp.dot(tri_ref[...], bs, preferred_element_type=F32))
    for a in range(2):
        lanes = slice(a * S5_HALF_STATE, (a + 1) * S5_HALF_STATE)
        cr = carry_re_ref[:, lanes]
        ci = carry_im_ref[:, lanes]
        df_re = dfwd_re_ref[:, lanes]
        df_im = dfwd_im_ref[:, lanes]
        for m in range(nsub):
            rows = slice(m * S5_SUB, (m + 1) * S5_SUB)
            zr = zs[a][rows, :S5_HALF_STATE] + cr
            zi = zs[a][rows, S5_HALF_STATE:] + ci
            xr = zr * df_re - zi * df_im
            xi = zr * df_im + zi * df_re
            cr = xr[S5_SUB - 1:S5_SUB, :]
            ci = xi[S5_SUB - 1:S5_SUB, :]
            x_ref[rows, lanes] = xr.astype(BF16)
            x_ref[rows, S5_PLANE + a * S5_HALF_STATE:S5_PLANE + (a + 1) * S5_HALF_STATE] = xi.astype(BF16)
            if m % 4 == 3:
                yield
        carry_re_ref[:, lanes] = _kept(cr, keep)
        carry_im_ref[:, lanes] = _kept(ci, keep)
    ys = []
    for a in range(2):
        xa = jnp.concatenate([x_ref[:, a * S5_HALF_STATE:(a + 1) * S5_HALF_STATE],
                              x_ref[:, S5_PLANE + a * S5_HALF_STATE:S5_PLANE + (a + 1) * S5_HALF_STATE]], axis=1)
        ys.append(jnp.dot(xa, cmat_ref[a], preferred_element_type=F32))
        yield
    y = jnp.concatenate(ys, axis=1) + dskip_ref[...] * u
    y = 0.5 * y * (1.0 + jnp.tanh(math.sqrt(2.0 / math.pi) * (y + 0.044715 * (y * y * y))))
    y_b = y.astype(BF16)
    yield
    glu = jnp.dot(y_b, wglu_ref[...], preferred_element_type=F32) + bglu_ref[...]
    emit((y * _sigmoid(glu) * _silu(gate)).astype(BF16))


def _ret_chunk(emit, keep, q_raw, k_raw, v, gate, cos_ref, sin_ref, qdec_ref, kdect_ref, dmat_ref, headmask_ref,
               kvmask_ref, cdec_ref, mask_l, mask_r, seg_ref, nw_ref, state_ref):
    half = RET_KEY_DIM // 2
    lane = lax.broadcasted_iota(jnp.int32, (CHUNK, RET_QK), 1)
    first_half = (lane % RET_KEY_DIM) < half
    cos = cos_ref[...]
    sin = sin_ref[...]

    def rotary(t):
        swapped = jnp.where(first_half, pltpu.roll(t, RET_QK - half, 1), pltpu.roll(t, half, 1))
        return t * cos + swapped * sin

    q = rotary(q_raw)
    yield
    k = rotary(k_raw) * (RET_KEY_DIM ** -0.5)
    v_b = v.astype(BF16)
    k_t = k.T
    kt_b = k_t.astype(BF16)
    yield
    q_b = q.astype(BF16)
    k_heads = jnp.concatenate([kt_b] * RET_HEADS, axis=1) * headmask_ref[...]
    qd_b = (q * qdec_ref[...]).astype(BF16)
    state_b = state_ref[...].astype(BF16)
    kd_b = (k_t * kdect_ref[...]).astype(BF16)
    yield
    scores = jnp.dot(q_b, k_heads, preferred_element_type=F32)
    y_cross = jnp.dot(qd_b, state_b, preferred_element_type=F32)
    kv = jnp.dot(kd_b, v_b, preferred_element_type=F32)
    yield
    state_ref[...] = _kept(state_ref[...] * cdec_ref[...] + kv * kvmask_ref[...], keep)
    y_parts = []
    for j in range(RET_HEADS // 2):
        sa = (scores[:, (2 * j) * CHUNK:(2 * j + 1) * CHUNK] * dmat_ref[2 * j]).astype(BF16)
        sb = (scores[:, (2 * j + 1) * CHUNK:(2 * j + 2) * CHUNK] * dmat_ref[2 * j + 1]).astype(BF16)
        rhs = _pair_rhs(v[:, j * LANES:(j + 1) * LANES], mask_l, mask_r)
        lhs = jnp.concatenate([sa, sb], axis=1)
        yield
        y_parts.append(jnp.dot(lhs, rhs, preferred_element_type=F32))
    y = jnp.concatenate(y_parts, axis=1) + y_cross
    yy_b = (y * y).astype(BF16)
    yield
    ms = jnp.dot(yy_b, seg_ref[...], preferred_element_type=F32)
    emit((y * lax.rsqrt(ms + EPS) * nw_ref[...] * _silu(gate)).astype(BF16))


N_SSD, N_S5, N_RET = 8, 10, 8


class _Filler:
    def __init__(self):
        self.units = []

    def add(self, fn):
        self.units.append(fn)

    def __call__(self, n=1):
        for _ in range(min(n, len(self.units))):
            self.units.pop(0)()


UNIT_COLS = 256


SUBSTEPS = 4


def _layer_kernel(final, nc, xin_ref, xres_ref, nw_ref, win_ref, *rest):
    pos_refs, rest = rest[:2 * SUBSTEPS], rest[2 * SUBSTEPS:]
    ssd_p = rest[:N_SSD]
    s5_p = rest[N_SSD:N_SSD + N_S5]
    ret_p = rest[N_SSD + N_S5:N_SSD + N_S5 + N_RET]
    wout_ref, fnw_ref, out_ref = rest[N_SSD + N_S5 + N_RET:N_SSD + N_S5 + N_RET + 3]
    scratch = rest[N_SSD + N_S5 + N_RET + 3:]
    proj = dict(zip([n for n, _ in PIECES], scratch[:len(PIECES)]))
    y_ref, h_ref, xprev_ref, ssd_state_ref, carry_re_ref, carry_im_ref, s5x_ref, ret_state_ref = scratch[len(PIECES):]
    s = pl.program_id(0)

    @pl.when(s == 0)
    def _():
        for ref in list(proj.values()) + [y_ref, xprev_ref, ssd_state_ref, carry_re_ref, carry_im_ref,
                                          ret_state_ref]:
            ref[...] = jnp.zeros_like(ref)

    convw, convb, dtb, alog, dskip, ssd_nw, tri3, pairmask = ssd_p
    qdec, kdect, dmat, headmask, kvmask, cdec, seg, ret_nw = ret_p
    mask_l = pairmask[0:1, :]
    mask_r = pairmask[1:2, :]

    def sub_step(k, keep):
        rows = slice(k * CHUNK, (k + 1) * CHUNK)
        wr, rd = k % 2, (k - 1) % 2
        y_rd, y_wr = k, (k - 1) % SUBSTEPS
        cos_ref, sin_ref = pos_refs[2 * k], pos_refs[2 * k + 1]
        fill = _Filler()

        def out_unit(c0):
            def run():
                out_ref[rows, c0:c0 + UNIT_COLS] = xres_ref[rows, c0:c0 + UNIT_COLS] + jnp.dot(
                    y_ref[y_rd], wout_ref[c0 // UNIT_COLS], preferred_element_type=F32)
            return run

        def in_unit(name, c0, c1):
            off, _ = PIECE_OFF[name]

            def run():
                proj[name][wr, :, c0:c1] = jnp.dot(h_ref[wr], win_ref[:, off + c0:off + c1],
                                                   preferred_element_type=F32)
            return run

        for c0 in range(0, D_MODEL, UNIT_COLS):
            fill.add(out_unit(c0))
        fill(1)
        x = xin_ref[rows, :]
        r = lax.rsqrt(jnp.mean(x * x, axis=-1, keepdims=True) + EPS)
        h_ref[wr] = (x * r * nw_ref[...]).astype(BF16)
        for name, width in PIECES:
            for c0 in range(0, width, UNIT_COLS):
                fill.add(in_unit(name, c0, min(c0 + UNIT_COLS, width)))

        def emit_to(c0, c1):
            def emit(value):
                y_ref[y_wr, :, c0:c1] = value
            return emit

        branches = [
            _ssd_chunk(emit_to(0, SSD_WIDTH), keep, proj["z"][rd], proj["xbc"][rd], proj["dt"][rd], convw, convb,
                       dtb, alog, dskip, ssd_nw, tri3, mask_l, mask_r, xprev_ref, ssd_state_ref),
            _s5_chunk(emit_to(SSD_WIDTH, SSD_WIDTH + S5_WIDTH), keep, proj["s5u"][rd], proj["s5g"][rd], *s5_p,
                      carry_re_ref, carry_im_ref, s5x_ref),
            _ret_chunk(emit_to(SSD_WIDTH + S5_WIDTH, MIX_WIDTH), keep, proj["q"][rd], proj["k"][rd], proj["v"][rd],
                       proj["rg"][rd], cos_ref, sin_ref, qdec, kdect, dmat, headmask, kvmask, cdec, mask_l, mask_r,
                       seg, ret_nw, ret_state_ref),
        ]
        while branches:
            for gen in list(branches):
                if next(gen, "done") == "done":
                    branches.remove(gen)
            fill(1)
        fill(len(fill.units))

    keep_first = jnp.where((SUBSTEPS * s) % nc == 0, 0.0, 1.0).astype(F32)
    for k in range(SUBSTEPS):
        sub_step(k, keep_first if k == 0 else None)
    if final:
        acc = out_ref[...]
        rr = lax.rsqrt(jnp.mean(acc * acc, axis=-1, keepdims=True) + EPS)
        out_ref[...] = acc * rr * fnw_ref[...]


def _ssd_constants():
    tri = np.tril(np.ones((CHUNK, CHUNK), np.float32))
    tri3 = np.concatenate([tri, tri, tri], axis=1)
    pair = np.zeros((2, LANES), np.float32)
    pair[0, :SSD_HEAD_DIM] = 1.0
    pair[1, SSD_HEAD_DIM:] = 1.0
    return jnp.asarray(tri3, BF16), jnp.asarray(pair, F32)


def _s5_tables(lam_re, lam_im, b_re, b_im, c_re, c_im, log_step):
    step = jnp.exp(log_step.astype(F32))[:, None]
    lam_re = lam_re.astype(F32)
    lam_im = lam_im.astype(F32)
    mag = jnp.exp(lam_re * step)
    ang = lam_im * step
    lb_re = mag * jnp.cos(ang)
    lb_im = mag * jnp.sin(ang)
    den = lam_re * lam_re + lam_im * lam_im
    f_re = ((lb_re - 1.0) * lam_re + lb_im * lam_im) / den
    f_im = (lb_im * lam_re - (lb_re - 1.0) * lam_im) / den
    bb_re = f_re[..., None] * b_re - f_im[..., None] * b_im
    bb_im = f_re[..., None] * b_im + f_im[..., None] * b_re
    eye = jnp.eye(S5_HALF_GROUPS, dtype=F32)

    def in_block(bb):
        bb = bb.reshape(2, S5_HALF_GROUPS, S5_STATE, S5_GROUP_CH)
        return jnp.einsum("agpc,gh->agchp", bb, eye).reshape(2, S5_HALF_CH, S5_HALF_STATE)

    def out_block(cc):
        cc = cc.reshape(2, S5_HALF_GROUPS, S5_GROUP_CH, S5_STATE)
        return jnp.einsum("agcp,gh->agphc", cc, eye).reshape(2, S5_HALF_STATE, S5_HALF_CH)

    bmat = jnp.concatenate([in_block(bb_re), in_block(bb_im)], axis=2).astype(BF16)
    bmat = bmat.reshape(2, S5_HALF_CH, 2 * S5_HALF_STATE // UNIT_COLS, UNIT_COLS).transpose(0, 2, 1, 3)
    cmat = jnp.concatenate([out_block(c_re), out_block(-c_im)], axis=1).astype(BF16)
    k = jnp.arange(1, S5_SUB + 1, dtype=F32)[:, None, None]
    la = (lam_re * step)[None]
    an = ang[None]
    dfwd_re = (jnp.exp(k * la) * jnp.cos(k * an)).reshape(S5_SUB, S5_PLANE)
    dfwd_im = (jnp.exp(k * la) * jnp.sin(k * an)).reshape(S5_SUB, S5_PLANE)
    dinv_re = (jnp.exp(-k * la) * jnp.cos(k * an)).reshape(S5_SUB, S5_PLANE)
    dinv_im = (-jnp.exp(-k * la) * jnp.sin(k * an)).reshape(S5_SUB, S5_PLANE)
    return bmat, cmat, dinv_re, dinv_im, dfwd_re, dfwd_im


def _ret_constants(l):
    pos = jnp.arange(l, dtype=F32)
    inv_freq = ROPE_BASE ** (-jnp.arange(0, RET_KEY_DIM, 2, dtype=F32) / RET_KEY_DIM)
    ang = pos[:, None] * inv_freq[None, :]
    cos, sin = jnp.cos(ang), jnp.sin(ang)
    cos_t = jnp.tile(jnp.concatenate([cos, cos], axis=1), (1, RET_HEADS))
    sin_t = jnp.tile(jnp.concatenate([-sin, sin], axis=1), (1, RET_HEADS))
    log_g = np.log1p(-np.exp2(-5.0 - np.arange(RET_HEADS, dtype=np.float64)))
    idx = np.arange(CHUNK, dtype=np.float64)
    diff = idx[:, None] - idx[None, :]
    dmat = np.where(diff >= 0, np.exp(np.maximum(diff, 0.0) * log_g[:, None, None]), 0.0)
    qdec = np.repeat(np.exp((idx + 1.0)[:, None] * log_g[None, :]), RET_KEY_DIM, axis=1)
    kdect = np.repeat(np.exp((CHUNK - 1.0 - idx)[:, None] * log_g[None, :]), RET_KEY_DIM, axis=1).T
    cdec = np.repeat(np.exp(CHUNK * log_g), RET_VAL_DIM).reshape(1, RET_WIDTH)
    headmask = np.kron(np.eye(RET_HEADS), np.ones((RET_KEY_DIM, CHUNK)))
    kvmask = np.kron(np.eye(RET_HEADS), np.ones((RET_KEY_DIM, RET_VAL_DIM)))
    seg = np.kron(np.eye(RET_HEADS), np.full((RET_VAL_DIM, RET_VAL_DIM), 1.0 / RET_VAL_DIM))
    return (cos_t, sin_t, jnp.asarray(qdec, F32), jnp.asarray(kdect, F32), jnp.asarray(dmat, F32),
            jnp.asarray(headmask, BF16), jnp.asarray(kvmask, F32), jnp.asarray(cdec, F32), jnp.asarray(seg, BF16))


def _pack_w_in(w):
    o = SSD_WIDTH + SSD_CONV_CH
    n_dt = SSD_HEADS
    w = w.astype(BF16)
    split = o + n_dt
    shift = DT_PAD - n_dt
    col = lax.broadcasted_iota(jnp.int32, w.shape[:2] + (IN_PACKED,), 2)
    stay = jnp.pad(w, ((0, 0), (0, 0), (0, shift)))
    moved = jnp.pad(w, ((0, 0), (0, 0), (shift, 0)))
    return jnp.where(col < split, stay, jnp.where(col >= split + shift, moved, jnp.zeros((), BF16)))


def _const_spec(shape):
    nd = len(shape)
    return pl.BlockSpec(shape, lambda *_: (0,) * nd, pipeline_mode=pl.Buffered(1))


class _PerLayer:
    def __init__(self, array):
        self.array = array

    def spec(self, layer):
        shape = self.array.shape[1:]
        return pl.BlockSpec((None,) + shape, lambda *_: (layer,) + (0,) * len(shape), pipeline_mode=pl.Buffered(1))


def _spec(p, layer):
    return p.spec(layer) if isinstance(p, _PerLayer) else _const_spec(p.shape)


def _array(p):
    return p.array if isinstance(p, _PerLayer) else p


def _layer(x, layer, final, consts, tail, cos_t, sin_t):
    b, l, d = x.shape
    nc = l // CHUNK
    assert nc % SUBSTEPS == 0, "sequences must hold a multiple of SUBSTEPS chunks"
    bps = nc // SUBSTEPS
    nb = b * bps

    def block_spec(lag):
        def index(s):
            t = jnp.clip(s - lag, 0, nb - 1)
            return (t // bps, t % bps, 0)
        return pl.BlockSpec((None, SUBSTEPS * CHUNK, d), index)

    def pos_spec(k):
        return pl.BlockSpec((CHUNK, RET_QK),
                            lambda s: (jnp.clip(SUBSTEPS * s + k - 1, 0, SUBSTEPS * nb - 1) % nc, 0))

    pos_specs = [pos_spec(k) for k in range(SUBSTEPS) for _ in range(2)]
    return pl.pallas_call(
        functools.partial(_layer_kernel, final, nc),
        grid=(nb + 1,),
        in_specs=[block_spec(0), block_spec(1)] + [_spec(p, layer) for p in consts] + pos_specs
        + [_spec(p, layer) for p in tail],
        out_specs=block_spec(1),
        out_shape=jax.ShapeDtypeStruct((b, l, d), F32),
        scratch_shapes=[pltpu.VMEM((2, CHUNK, w), F32) for _, w in PIECES]
        + [pltpu.VMEM((SUBSTEPS, CHUNK, MIX_WIDTH), BF16),
           pltpu.VMEM((2, CHUNK, D_MODEL), BF16),
           pltpu.VMEM((SUBLANES, SSD_CONV_CH), F32),
           pltpu.VMEM((SSD_STATE, SSD_WIDTH), F32),
           pltpu.VMEM((1, S5_PLANE), F32), pltpu.VMEM((1, S5_PLANE), F32),
           pltpu.VMEM((CHUNK, 2 * S5_PLANE), BF16),
           pltpu.VMEM((RET_QK, RET_WIDTH), F32)],
        compiler_params=pltpu.CompilerParams(dimension_semantics=("arbitrary",), vmem_limit_bytes=VMEM_LIMIT),
        name="layer",
    )(x, x, *map(_array, consts), *([cos_t, sin_t] * SUBSTEPS), *map(_array, tail))


def kernel(x, norm_w, w_in, conv_w, conv_b, dt_bias, a_log, d_ssd, ssd_norm_w, s5_lambda_re, s5_lambda_im,
           s5_b_re, s5_b_im, s5_c_re, s5_c_im, s5_d, s5_log_step, s5_w_glu, s5_b_glu, ret_norm_w, w_out,
           final_norm_w):
    b, l, d = x.shape
    depth = w_in.shape[0]
    tri3, pair = _ssd_constants()
    cos_t, sin_t, qdec, kdect, dmat, headmask, kvmask, cdec, seg = _ret_constants(l)
    s5tri = jnp.asarray(np.kron(np.eye(CHUNK // S5_SUB, dtype=np.float32),
                                np.tril(np.ones((S5_SUB, S5_SUB), np.float32))), BF16)
    row = lambda p: _PerLayer(p.reshape(depth, 1, -1))
    pad_row = lambda p: _PerLayer(jnp.pad(p, ((0, 0), (0, DT_PAD - p.shape[-1]))).reshape(depth, 1, DT_PAD))
    consts = [row(norm_w), _PerLayer(_pack_w_in(w_in))]
    ssd_params = [_PerLayer(conv_w), row(conv_b), pad_row(dt_bias), pad_row(a_log),
                  row(jnp.repeat(d_ssd, SSD_HEAD_DIM, axis=-1)), row(ssd_norm_w), tri3, pair]
    s5_tables = jax.vmap(_s5_tables)(s5_lambda_re, s5_lambda_im, s5_b_re, s5_b_im, s5_c_re, s5_c_im, s5_log_step)
    s5_params = [_PerLayer(t) for t in s5_tables] + [s5tri, row(s5_d), _PerLayer(s5_w_glu.astype(BF16)),
                                                      row(s5_b_glu)]
    ret_params = [qdec, kdect, dmat, headmask, kvmask, cdec, seg, row(ret_norm_w)]
    w_out_tiles = w_out.astype(BF16).reshape(depth, MIX_WIDTH, d // UNIT_COLS, UNIT_COLS).transpose(0, 2, 1, 3)
    tail = ssd_params + s5_params + ret_params + [_PerLayer(w_out_tiles), final_norm_w.reshape(1, d)]
    h_res = x.astype(F32)
    for i in range(depth):
        h_res = _layer(h_res, i, i == depth - 1, consts, tail, cos_t, sin_t)
    return h_res.astype(x.dtype)
```

```python
import functools
import math

import numpy as np
import jax
import jax.numpy as jnp
from jax import lax
from jax.experimental import pallas as pl
from jax.experimental.pallas import tpu as pltpu

F32 = jnp.float32
BF16 = jnp.bfloat16

D_MODEL = 1024
CHUNK = 128
EPS = 1e-6
SSD_HEADS = 16
SSD_HEAD_DIM = 64
SSD_WIDTH = SSD_HEADS * SSD_HEAD_DIM
SSD_GROUPS = 2
SSD_STATE = 128
CONV_WIDTH = 4
SSD_BC = SSD_GROUPS * SSD_STATE
SSD_CONV_CH = SSD_WIDTH + 2 * SSD_BC
S5_GROUP_CH = 16
S5_GROUPS = 32
S5_WIDTH = S5_GROUPS * S5_GROUP_CH
S5_STATE = 64
S5_HALF_GROUPS = S5_GROUPS // 2
S5_HALF_CH = S5_WIDTH // 2
S5_HALF_STATE = S5_HALF_GROUPS * S5_STATE
S5_PLANE = S5_GROUPS * S5_STATE
S5_SUB = 16
RET_HEADS = 8
RET_KEY_DIM = 32
RET_VAL_DIM = 64
RET_QK = RET_HEADS * RET_KEY_DIM
RET_WIDTH = RET_HEADS * RET_VAL_DIM
ROPE_BASE = 10000.0
MIX_WIDTH = SSD_WIDTH + S5_WIDTH + RET_WIDTH
LANES = 128
SUBLANES = 8
DT_PAD = LANES
PIECES = (("z", SSD_WIDTH), ("xbc", SSD_CONV_CH), ("dt", DT_PAD), ("s5g", S5_WIDTH), ("s5u", S5_WIDTH),
          ("q", RET_QK), ("k", RET_QK), ("v", RET_WIDTH), ("rg", RET_WIDTH))
PIECE_OFF = {}
_o = 0
for _n, _w in PIECES:
    PIECE_OFF[_n] = (_o, _w)
    _o += _w
IN_PACKED = _o
VMEM_LIMIT = 56 * 1024 * 1024


def _silu(x):
    h = 0.5 * x
    return h + h * jnp.tanh(h)


def _sigmoid(x):
    return 0.5 + 0.5 * jnp.tanh(0.5 * x)


def _split3(x):
    p1 = x.astype(BF16)
    r1 = x - p1.astype(F32)
    p2 = r1.astype(BF16)
    p3 = (r1 - p2.astype(F32)).astype(BF16)
    return p1, p2, p3


def _pair_rhs(xp, mask_l, mask_r):
    return jnp.concatenate([(xp * mask_l).astype(BF16), (xp * mask_r).astype(BF16)], axis=0)


def _kept(value, keep):
    return value if keep is None else value * keep


def _ssd_chunk(emit, keep, z, xbc_raw, dt_raw, convw_ref, convb_ref, dtb_ref, alog_ref, dskip_ref, nw_ref,
               tri3_ref, mask_l, mask_r, xprev_ref, state_ref):
    xprev = xprev_ref[...]
    row8 = lax.broadcasted_iota(jnp.int32, (SUBLANES, UNIT_COLS), 0)
    blocks = []
    for c0 in range(0, SSD_CONV_CH, UNIT_COLS):
        cols = slice(c0, c0 + UNIT_COLS)
        xb = xbc_raw[:, cols]
        acc = convb_ref[:, cols] + convw_ref[CONV_WIDTH - 1:CONV_WIDTH, cols] * xb
        for j in range(1, CONV_WIDTH):
            rolled = pltpu.roll(xb, j, 0)
            head = jnp.where(row8 < j, pltpu.roll(xprev[:, cols], j, 0), rolled[0:SUBLANES, :])
            shifted = jnp.concatenate([head, rolled[SUBLANES:, :]], axis=0)
            acc = acc + convw_ref[CONV_WIDTH - 1 - j:CONV_WIDTH - j, cols] * shifted
        blocks.append(_silu(acc))
        yield
    xprev_ref[...] = _kept(xbc_raw[CHUNK - SUBLANES:, :], keep)
    xbc = jnp.concatenate(blocks, axis=1)
    xs = xbc[:, :SSD_WIDTH]
    bm = xbc[:, SSD_WIDTH:SSD_WIDTH + SSD_BC]
    cm = xbc[:, SSD_WIDTH + SSD_BC:]

    dtr = dt_raw + dtb_ref[...]
    dt = jnp.maximum(dtr, 0.0) + jnp.log1p(jnp.exp(-jnp.abs(dtr)))
    a = -jnp.exp(alog_ref[...])
    dta3 = jnp.concatenate(_split3(dt * a), axis=0)
    yield
    acum = jnp.dot(tri3_ref[...], dta3, preferred_element_type=F32)
    acum_t = acum.T
    dt_t = dt.T
    ea = jnp.exp(acum)
    wgt = jnp.exp(acum[CHUNK - 1:CHUNK, :] - acum) * dt

    left_head = lax.broadcasted_iota(jnp.int32, (CHUNK, LANES), 1) < SSD_HEAD_DIM

    def expand(v):
        tiles = []
        for j in range(SSD_HEADS // 2):
            tiles.append(jnp.where(left_head, jnp.broadcast_to(v[:, 2 * j:2 * j + 1], (CHUNK, LANES)),
                                   jnp.broadcast_to(v[:, 2 * j + 1:2 * j + 2], (CHUNK, LANES))))
        return jnp.concatenate(tiles, axis=1)

    wgt_x = expand(wgt)
    ea_x = expand(ea)
    yield

    xw_b = (xs * wgt_x).astype(BF16)
    row = lax.broadcasted_iota(jnp.int32, (CHUNK, CHUNK), 0)
    col = lax.broadcasted_iota(jnp.int32, (CHUNK, CHUNK), 1)
    causal = row >= col
    hg = SSD_HEADS // SSD_GROUPS
    gw = hg * SSD_HEAD_DIM
    y_parts = []
    s_parts = []
    for g in range(SSD_GROUPS):
        bm_g = bm[:, g * SSD_STATE:(g + 1) * SSD_STATE]
        cm_b = cm[:, g * SSD_STATE:(g + 1) * SSD_STATE].astype(BF16)
        bm_b = bm_g.astype(BF16)
        bm_tb = bm_g.T.astype(BF16)
        state_b = state_ref[:, g * gw:(g + 1) * gw].astype(BF16)
        yield
        cb = lax.dot_general(cm_b, bm_b, (((1,), (1,)), ((), ())), preferred_element_type=F32)
        s_parts.append(jnp.dot(bm_tb, xw_b[:, g * gw:(g + 1) * gw], preferred_element_type=F32))
        y_off = jnp.dot(cm_b, state_b, preferred_element_type=F32)
        scores = []
        for r in range(hg):
            h = g * hg + r
            seg = acum[:, h:h + 1] - acum_t[h:h + 1, :]
            decay = jnp.exp(jnp.where(causal, seg, -jnp.inf))
            scores.append((cb * decay * dt_t[h:h + 1, :]).astype(BF16))
            if r % 2 == 1:
                j = r // 2
                lo_ = g * gw + j * LANES
                rhs = _pair_rhs(xs[:, lo_:lo_ + LANES], mask_l, mask_r)
                lhs = jnp.concatenate([scores[2 * j], scores[2 * j + 1]], axis=1)
                yield
                y_parts.append(jnp.dot(lhs, rhs, preferred_element_type=F32)
                               + y_off[:, j * LANES:(j + 1) * LANES] * ea_x[:, lo_:lo_ + LANES])
    state_ref[...] = _kept(state_ref[...] * ea_x[CHUNK - 1:CHUNK, :] + jnp.concatenate(s_parts, axis=1), keep)

    outs = []
    for g in range(SSD_GROUPS):
        cols = slice(g * gw, (g + 1) * gw)
        yg = jnp.concatenate(y_parts[g * (hg // 2):(g + 1) * (hg // 2)], axis=1) + xs[:, cols] * dskip_ref[:, cols]
        yg = yg * _silu(z[:, cols])
        ms = jnp.mean(yg * yg, axis=-1, keepdims=True)
        outs.append(yg * lax.rsqrt(ms + EPS) * nw_ref[:, cols])
        yield
    emit(jnp.concatenate(outs, axis=1).astype(BF16))


def _s5_chunk(emit, keep, u, gate, bmat_ref, cmat_ref, dinv_re_ref, dinv_im_ref, dfwd_re_ref, dfwd_im_ref,
              tri_ref, dskip_ref, wglu_ref, bglu_ref, carry_re_ref, carry_im_ref, x_ref):
    u_b = u.astype(BF16)
    nsub = CHUNK // S5_SUB
    halves = []
    for a in range(2):
        u_a = u_b[:, a * S5_HALF_CH:(a + 1) * S5_HALF_CH]
        halves.append(jnp.concatenate([jnp.dot(u_a, bmat_ref[a, j], preferred_element_type=F32)
                                       for j in range(2 * S5_HALF_STATE // UNIT_COLS)], axis=1))
        yield
    zs = []
    for a in range(2):
        lanes = slice(a * S5_HALF_STATE, (a + 1) * S5_HALF_STATE)
        bu_re = halves[a][:, :S5_HALF_STATE]
        bu_im = halves[a][:, S5_HALF_STATE:]
        di_re = jnp.concatenate([dinv_re_ref[:, lanes]] * nsub, axis=0)
        di_im = jnp.concatenate([dinv_im_ref[:, lanes]] * nsub, axis=0)
        bs = jnp.concatenate([bu_re * di_re - bu_im * di_im, bu_re * di_im + bu_im * di_re], axis=1).astype(BF16)
        yield
        zs.append(jnp.dot(tri_ref[...], bs, preferred_element_type=F32))
    for a in range(2):
        lanes = slice(a * S5_HALF_STATE, (a + 1) * S5_HALF_STATE)
        cr = carry_re_ref[:, lanes]
        ci = carry_im_ref[:, lanes]
        df_re = dfwd_re_ref[:, lanes]
        df_im = dfwd_im_ref[:, lanes]
        for m in range(nsub):
            rows = slice(m * S5_SUB, (m + 1) * S5_SUB)
            zr = zs[a][rows, :S5_HALF_STATE] + cr
            zi = zs[a][rows, S5_HALF_STATE:] + ci
            xr = zr * df_re - zi * df_im
            xi = zr * df_im + zi * df_re
            cr = xr[S5_SUB - 1:S5_SUB, :]
            ci = xi[S5_SUB - 1:S5_SUB, :]
            x_ref[rows, lanes] = xr.astype(BF16)
            x_ref[rows, S5_PLANE + a * S5_HALF_STATE:S5_PLANE + (a + 1) * S5_HALF_STATE] = xi.astype(BF16)
            if m % 4 == 3:
                yield
        carry_re_ref[:, lanes] = _kept(cr, keep)
        carry_im_ref[:, lanes] = _kept(ci, keep)
    ys = []
    for a in range(2):
        xa = jnp.concatenate([x_ref[:, a * S5_HALF_STATE:(a + 1) * S5_HALF_STATE],
                              x_ref[:, S5_PLANE + a * S5_HALF_STATE:S5_PLANE + (a + 1) * S5_HALF_STATE]], axis=1)
        ys.append(jnp.dot(xa, cmat_ref[a], preferred_element_type=F32))
        yield
    y = jnp.concatenate(ys, axis=1) + dskip_ref[...] * u
    y = 0.5 * y * (1.0 + jnp.tanh(math.sqrt(2.0 / math.pi) * (y + 0.044715 * (y * y * y))))
    y_b = y.astype(BF16)
    yield
    glu = jnp.dot(y_b, wglu_ref[...], preferred_element_type=F32) + bglu_ref[...]
    emit((y * _sigmoid(glu) * _silu(gate)).astype(BF16))


def _ret_chunk(emit, keep, q_raw, k_raw, v, gate, cos_ref, sin_ref, qdec_ref, kdect_ref, dmat_ref, headmask_ref,
               kvmask_ref, cdec_ref, mask_l, mask_r, seg_ref, nw_ref, state_ref):
    half = RET_KEY_DIM // 2
    lane = lax.broadcasted_iota(jnp.int32, (CHUNK, RET_QK), 1)
    first_half = (lane % RET_KEY_DIM) < half
    cos = cos_ref[...]
    sin = sin_ref[...]

    def rotary(t):
        swapped = jnp.where(first_half, pltpu.roll(t, RET_QK - half, 1), pltpu.roll(t, half, 1))
        return t * cos + swapped * sin

    q = rotary(q_raw)
    yield
    k = rotary(k_raw) * (RET_KEY_DIM ** -0.5)
    v_b = v.astype(BF16)
    k_t = k.T
    kt_b = k_t.astype(BF16)
    yield
    q_b = q.astype(BF16)
    k_heads = jnp.concatenate([kt_b] * RET_HEADS, axis=1) * headmask_ref[...]
    qd_b = (q * qdec_ref[...]).astype(BF16)
    state_b = state_ref[...].astype(BF16)
    kd_b = (k_t * kdect_ref[...]).astype(BF16)
    yield
    scores = jnp.dot(q_b, k_heads, preferred_element_type=F32)
    y_cross = jnp.dot(qd_b, state_b, preferred_element_type=F32)
    kv = jnp.dot(kd_b, v_b, preferred_element_type=F32)
    yield
    state_ref[...] = _kept(state_ref[...] * cdec_ref[...] + kv * kvmask_ref[...], keep)
    y_parts = []
    for j in range(RET_HEADS // 2):
        sa = (scores[:, (2 * j) * CHUNK:(2 * j + 1) * CHUNK] * dmat_ref[2 * j]).astype(BF16)
        sb = (scores[:, (2 * j + 1) * CHUNK:(2 * j + 2) * CHUNK] * dmat_ref[2 * j + 1]).astype(BF16)
        rhs = _pair_rhs(v[:, j * LANES:(j + 1) * LANES], mask_l, mask_r)
        lhs = jnp.concatenate([sa, sb], axis=1)
        yield
        y_parts.append(jnp.dot(lhs, rhs, preferred_element_type=F32))
    y = jnp.concatenate(y_parts, axis=1) + y_cross
    yy_b = (y * y).astype(BF16)
    yield
    ms = jnp.dot(yy_b, seg_ref[...], preferred_element_type=F32)
    emit((y * lax.rsqrt(ms + EPS) * nw_ref[...] * _silu(gate)).astype(BF16))


N_SSD, N_S5, N_RET = 8, 10, 8


class _Filler:
    def __init__(self):
        self.units = []

    def add(self, fn):
        self.units.append(fn)

    def __call__(self, n=1):
        for _ in range(min(n, len(self.units))):
            self.units.pop(0)()


UNIT_COLS = 256


def _layer_kernel(final, nc, xin_ref, nw_ref, win_ref, cos_a_ref, sin_a_ref, cos_b_ref, sin_b_ref, *rest):
    ssd_p = rest[:N_SSD]
    s5_p = rest[N_SSD:N_SSD + N_S5]
    ret_p = rest[N_SSD + N_S5:N_SSD + N_S5 + N_RET]
    wout_ref, fnw_ref, out_ref = rest[N_SSD + N_S5 + N_RET:N_SSD + N_S5 + N_RET + 3]
    scratch = rest[N_SSD + N_S5 + N_RET + 3:]
    proj = dict(zip([n for n, _ in PIECES], scratch[:len(PIECES)]))
    (y_ref, h_ref, xres_ref, xprev_ref, ssd_state_ref, carry_re_ref, carry_im_ref, s5x_ref,
     ret_state_ref) = scratch[len(PIECES):]
    s = pl.program_id(0)

    @pl.when(s == 0)
    def _():
        for ref in list(proj.values()) + [y_ref, xres_ref, xprev_ref, ssd_state_ref, carry_re_ref, carry_im_ref,
                                          ret_state_ref]:
            ref[...] = jnp.zeros_like(ref)

    convw, convb, dtb, alog, dskip, ssd_nw, tri3, pairmask = ssd_p
    qdec, kdect, dmat, headmask, kvmask, cdec, seg, ret_nw = ret_p
    mask_l = pairmask[0:1, :]
    mask_r = pairmask[1:2, :]

    def half_step(rows, wr, rd, cos_ref, sin_ref, keep):
        fill = _Filler()

        def out_unit(c0):
            def run():
                out_ref[rows, c0:c0 + UNIT_COLS] = xres_ref[rows, c0:c0 + UNIT_COLS] + jnp.dot(
                    y_ref[wr], wout_ref[c0 // UNIT_COLS], preferred_element_type=F32)
            return run

        def in_unit(name, c0, c1):
            off, _ = PIECE_OFF[name]

            def run():
                proj[name][wr, :, c0:c1] = jnp.dot(h_ref[wr], win_ref[:, off + c0:off + c1],
                                                   preferred_element_type=F32)
            return run

        for c0 in range(0, D_MODEL, UNIT_COLS):
            fill.add(out_unit(c0))
        fill(1)
        x = xin_ref[rows, :]
        r = lax.rsqrt(jnp.mean(x * x, axis=-1, keepdims=True) + EPS)
        h_ref[wr] = (x * r * nw_ref[...]).astype(BF16)
        for name, width in PIECES:
            for c0 in range(0, width, UNIT_COLS):
                fill.add(in_unit(name, c0, min(c0 + UNIT_COLS, width)))

        def emit_to(c0, c1):
            def emit(value):
                y_ref[rd, :, c0:c1] = value
            return emit

        branches = [
            _ssd_chunk(emit_to(0, SSD_WIDTH), keep, proj["z"][rd], proj["xbc"][rd], proj["dt"][rd], convw, convb,
                       dtb, alog, dskip, ssd_nw, tri3, mask_l, mask_r, xprev_ref, ssd_state_ref),
            _s5_chunk(emit_to(SSD_WIDTH, SSD_WIDTH + S5_WIDTH), keep, proj["s5u"][rd], proj["s5g"][rd], *s5_p,
                      carry_re_ref, carry_im_ref, s5x_ref),
            _ret_chunk(emit_to(SSD_WIDTH + S5_WIDTH, MIX_WIDTH), keep, proj["q"][rd], proj["k"][rd], proj["v"][rd],
                       proj["rg"][rd], cos_ref, sin_ref, qdec, kdect, dmat, headmask, kvmask, cdec, mask_l, mask_r,
                       seg, ret_nw, ret_state_ref),
        ]
        while branches:
            for gen in list(branches):
                if next(gen, "done") == "done":
                    branches.remove(gen)
            fill(1)
        fill(len(fill.units))

    keep_a = jnp.where((2 * s) % nc == 0, 0.0, 1.0).astype(F32)
    half_step(slice(0, CHUNK), 0, 1, cos_a_ref, sin_a_ref, keep_a)
    half_step(slice(CHUNK, 2 * CHUNK), 1, 0, cos_b_ref, sin_b_ref, None)
    xres_ref[...] = xin_ref[...]
    if final:
        acc = out_ref[...]
        rr = lax.rsqrt(jnp.mean(acc * acc, axis=-1, keepdims=True) + EPS)
        out_ref[...] = acc * rr * fnw_ref[...]


def _ssd_constants():
    tri = np.tril(np.ones((CHUNK, CHUNK), np.float32))
    tri3 = np.concatenate([tri, tri, tri], axis=1)
    pair = np.zeros((2, LANES), np.float32)
    pair[0, :SSD_HEAD_DIM] = 1.0
    pair[1, SSD_HEAD_DIM:] = 1.0
    return jnp.asarray(tri3, BF16), jnp.asarray(pair, F32)


def _s5_tables(lam_re, lam_im, b_re, b_im, c_re, c_im, log_step):
    step = jnp.exp(log_step.astype(F32))[:, None]
    lam_re = lam_re.astype(F32)
    lam_im = lam_im.astype(F32)
    mag = jnp.exp(lam_re * step)
    ang = lam_im * step
    lb_re = mag * jnp.cos(ang)
    lb_im = mag * jnp.sin(ang)
    den = lam_re * lam_re + lam_im * lam_im
    f_re = ((lb_re - 1.0) * lam_re + lb_im * lam_im) / den
    f_im = (lb_im * lam_re - (lb_re - 1.0) * lam_im) / den
    bb_re = f_re[..., None] * b_re - f_im[..., None] * b_im
    bb_im = f_re[..., None] * b_im + f_im[..., None] * b_re
    eye = jnp.eye(S5_HALF_GROUPS, dtype=F32)

    def in_block(bb):
        bb = bb.reshape(2, S5_HALF_GROUPS, S5_STATE, S5_GROUP_CH)
        return jnp.einsum("agpc,gh->agchp", bb, eye).reshape(2, S5_HALF_CH, S5_HALF_STATE)

    def out_block(cc):
        cc = cc.reshape(2, S5_HALF_GROUPS, S5_GROUP_CH, S5_STATE)
        return jnp.einsum("agcp,gh->agphc", cc, eye).reshape(2, S5_HALF_STATE, S5_HALF_CH)

    bmat = jnp.concatenate([in_block(bb_re), in_block(bb_im)], axis=2).astype(BF16)
    bmat = bmat.reshape(2, S5_HALF_CH, 2 * S5_HALF_STATE // UNIT_COLS, UNIT_COLS).transpose(0, 2, 1, 3)
    cmat = jnp.concatenate([out_block(c_re), out_block(-c_im)], axis=1).astype(BF16)
    k = jnp.arange(1, S5_SUB + 1, dtype=F32)[:, None, None]
    la = (lam_re * step)[None]
    an = ang[None]
    dfwd_re = (jnp.exp(k * la) * jnp.cos(k * an)).reshape(S5_SUB, S5_PLANE)
    dfwd_im = (jnp.exp(k * la) * jnp.sin(k * an)).reshape(S5_SUB, S5_PLANE)
    dinv_re = (jnp.exp(-k * la) * jnp.cos(k * an)).reshape(S5_SUB, S5_PLANE)
    dinv_im = (-jnp.exp(-k * la) * jnp.sin(k * an)).reshape(S5_SUB, S5_PLANE)
    return bmat, cmat, dinv_re, dinv_im, dfwd_re, dfwd_im


def _ret_constants(l):
    pos = jnp.arange(l, dtype=F32)
    inv_freq = ROPE_BASE ** (-jnp.arange(0, RET_KEY_DIM, 2, dtype=F32) / RET_KEY_DIM)
    ang = pos[:, None] * inv_freq[None, :]
    cos, sin = jnp.cos(ang), jnp.sin(ang)
    cos_t = jnp.tile(jnp.concatenate([cos, cos], axis=1), (1, RET_HEADS))
    sin_t = jnp.tile(jnp.concatenate([-sin, sin], axis=1), (1, RET_HEADS))
    log_g = np.log1p(-np.exp2(-5.0 - np.arange(RET_HEADS, dtype=np.float64)))
    idx = np.arange(CHUNK, dtype=np.float64)
    diff = idx[:, None] - idx[None, :]
    dmat = np.where(diff >= 0, np.exp(np.maximum(diff, 0.0) * log_g[:, None, None]), 0.0)
    qdec = np.repeat(np.exp((idx + 1.0)[:, None] * log_g[None, :]), RET_KEY_DIM, axis=1)
    kdect = np.repeat(np.exp((CHUNK - 1.0 - idx)[:, None] * log_g[None, :]), RET_KEY_DIM, axis=1).T
    cdec = np.repeat(np.exp(CHUNK * log_g), RET_VAL_DIM).reshape(1, RET_WIDTH)
    headmask = np.kron(np.eye(RET_HEADS), np.ones((RET_KEY_DIM, CHUNK)))
    kvmask = np.kron(np.eye(RET_HEADS), np.ones((RET_KEY_DIM, RET_VAL_DIM)))
    seg = np.kron(np.eye(RET_HEADS), np.full((RET_VAL_DIM, RET_VAL_DIM), 1.0 / RET_VAL_DIM))
    return (cos_t, sin_t, jnp.asarray(qdec, F32), jnp.asarray(kdect, F32), jnp.asarray(dmat, F32),
            jnp.asarray(headmask, BF16), jnp.asarray(kvmask, F32), jnp.asarray(cdec, F32), jnp.asarray(seg, BF16))


def _pack_w_in(w):
    o = SSD_WIDTH + SSD_CONV_CH
    n_dt = SSD_HEADS
    w = w.astype(BF16)
    split = o + n_dt
    head = jnp.pad(w[..., :split], ((0, 0), (0, 0), (0, IN_PACKED - split)))
    tail = jnp.pad(w[..., split:], ((0, 0), (0, 0), (split + DT_PAD - n_dt, 0)))
    return head + tail


def _const_spec(shape):
    nd = len(shape)
    return pl.BlockSpec(shape, lambda *_: (0,) * nd, pipeline_mode=pl.Buffered(1))


class _PerLayer:
    def __init__(self, array):
        self.array = array

    def spec(self, layer):
        shape = self.array.shape[1:]
        return pl.BlockSpec((None,) + shape, lambda *_: (layer,) + (0,) * len(shape), pipeline_mode=pl.Buffered(1))


def _spec(p, layer):
    return p.spec(layer) if isinstance(p, _PerLayer) else _const_spec(p.shape)


def _array(p):
    return p.array if isinstance(p, _PerLayer) else p


def _layer(x, layer, final, consts, tail, cos_t, sin_t):
    b, l, d = x.shape
    nc = l // CHUNK
    assert nc % 2 == 0, "sequences must hold an even number of chunks"
    nb = b * nc // 2

    def block_spec(lag):
        def index(s):
            t = jnp.clip(s - lag, 0, nb - 1)
            return (t // (nc // 2), t % (nc // 2), 0)
        return pl.BlockSpec((None, 2 * CHUNK, d), index)

    def pos_spec(lag):
        return pl.BlockSpec((CHUNK, RET_QK), lambda s: (jnp.clip(2 * s - lag, 0, 2 * nb - 1) % nc, 0))

    return pl.pallas_call(
        functools.partial(_layer_kernel, final, nc),
        grid=(nb + 1,),
        in_specs=[block_spec(0)] + [_spec(p, layer) for p in consts]
        + [pos_spec(1), pos_spec(1), pos_spec(0), pos_spec(0)] + [_spec(p, layer) for p in tail],
        out_specs=block_spec(1),
        out_shape=jax.ShapeDtypeStruct((b, l, d), F32),
        scratch_shapes=[pltpu.VMEM((2, CHUNK, w), F32) for _, w in PIECES]
        + [pltpu.VMEM((2, CHUNK, MIX_WIDTH), BF16),
           pltpu.VMEM((2, CHUNK, D_MODEL), BF16),
           pltpu.VMEM((2 * CHUNK, D_MODEL), F32),
           pltpu.VMEM((SUBLANES, SSD_CONV_CH), F32),
           pltpu.VMEM((SSD_STATE, SSD_WIDTH), F32),
           pltpu.VMEM((1, S5_PLANE), F32), pltpu.VMEM((1, S5_PLANE), F32),
           pltpu.VMEM((CHUNK, 2 * S5_PLANE), BF16),
           pltpu.VMEM((RET_QK, RET_WIDTH), F32)],
        compiler_params=pltpu.CompilerParams(dimension_semantics=("arbitrary",), vmem_limit_bytes=VMEM_LIMIT),
        name="layer",
    )(x, *map(_array, consts), cos_t, sin_t, cos_t, sin_t, *map(_array, tail))


def kernel(x, norm_w, w_in, conv_w, conv_b, dt_bias, a_log, d_ssd, ssd_norm_w, s5_lambda_re, s5_lambda_im,
           s5_b_re, s5_b_im, s5_c_re, s5_c_im, s5_d, s5_log_step, s5_w_glu, s5_b_glu, ret_norm_w, w_out,
           final_norm_w):
    b, l, d = x.shape
    depth = w_in.shape[0]
    tri3, pair = _ssd_constants()
    cos_t, sin_t, qdec, kdect, dmat, headmask, kvmask, cdec, seg = _ret_constants(l)
    s5tri = jnp.asarray(np.kron(np.eye(CHUNK // S5_SUB, dtype=np.float32),
                                np.tril(np.ones((S5_SUB, S5_SUB), np.float32))), BF16)
    row = lambda p: _PerLayer(p.reshape(depth, 1, -1))
    pad_row = lambda p: _PerLayer(jnp.pad(p, ((0, 0), (0, DT_PAD - p.shape[-1]))).reshape(depth, 1, DT_PAD))
    consts = [row(norm_w), _PerLayer(_pack_w_in(w_in))]
    ssd_params = [_PerLayer(conv_w), row(conv_b), pad_row(dt_bias), pad_row(a_log),
                  row(jnp.repeat(d_ssd, SSD_HEAD_DIM, axis=-1)), row(ssd_norm_w), tri3, pair]
    s5_tables = jax.vmap(_s5_tables)(s5_lambda_re, s5_lambda_im, s5_b_re, s5_b_im, s5_c_re, s5_c_im, s5_log_step)
    s5_params = [_PerLayer(t) for t in s5_tables] + [s5tri, row(s5_d), _PerLayer(s5_w_glu.astype(BF16)),
                                                      row(s5_b_glu)]
    ret_params = [qdec, kdect, dmat, headmask, kvmask, cdec, seg, row(ret_norm_w)]
    w_out_tiles = w_out.astype(BF16).reshape(depth, MIX_WIDTH, d // UNIT_COLS, UNIT_COLS).transpose(0, 2, 1, 3)
    tail = ssd_params + s5_params + ret_params + [_PerLayer(w_out_tiles), final_norm_w.reshape(1, d)]
    h_res = x.astype(F32)
    for i in range(depth):
        h_res = _layer(h_res, i, i == depth - 1, consts, tail, cos_t, sin_t)
    return h_res.astype(x.dtype)
```
